```python
import math
import jax, jax.numpy as jnp
from jax import lax
import numpy as np

D_MODEL = 4096
BATCH = 4
SEQ = 2048
DEPTH = 2
DEC_BATCH = 128
DEC_SEQ = 4
PAST_LEN = 16384
PAGE_SIZE = 128

MIX_HALF = D_MODEL // 2
DK_A = 128
DV_A = 128
H_A = MIX_HALF // DV_A
DK_B = 128
DV_B = 128
H_B = MIX_HALF // DV_B
CONV_W = 4
C_B = 2 * H_B * DK_B + H_B * DV_B
RG_WIDTH = MIX_HALF
RG_BW = 128
RG_BLOCKS = RG_WIDTH // RG_BW
RG_C = 8.0
DK_D = 128
DV_D = 256
H_D = MIX_HALF // DV_D
D_FF = 11008
CHUNK_A = 16
CHUNK_B = 64
CHUNK_D = 64
ROPE_BASE = 10000.0
LN_EPS = 1e-5
RMS_EPS = 1e-6
ALPHA = (2 * DEPTH) ** 0.25
BETA_INIT = (8 * DEPTH) ** -0.25

IN0_SIZES = (H_A * DK_A, H_A * DK_A, H_A * DV_A, H_A * DV_A, C_B, H_B * DV_B, H_B, H_B)
IN0 = sum(IN0_SIZES)
OUT0 = H_A * DV_A + H_B * DV_B
IN1_SIZES = (RG_WIDTH, RG_WIDTH, H_D * DK_D, H_D * DK_D, H_D * DV_D, H_D * DV_D)
IN1 = sum(IN1_SIZES)
OUT1 = RG_WIDTH + H_D * DV_D

kernel_name = 'hybrid_hgrn2_gdn_rglru_retnet_macaron_step'

F32 = jnp.float32


def split_cols(u, sizes):
    return jnp.split(u, [int(s) for s in np.cumsum(sizes)[:-1]], axis=-1)


def layer_norm(z, g, b):
    mu = jnp.mean(z, -1, keepdims=True)
    var = jnp.mean(jnp.square(z - mu), -1, keepdims=True)
    return (z - mu) * lax.rsqrt(var + LN_EPS) * g.astype(F32) + b.astype(F32)


def rms_norm(z, g):
    return z * lax.rsqrt(jnp.mean(jnp.square(z), -1, keepdims=True) + RMS_EPS) * g.astype(F32)


def group_norm(z):
    mu = jnp.mean(z, -1, keepdims=True)
    var = jnp.mean(jnp.square(z - mu), -1, keepdims=True)
    return (z - mu) * lax.rsqrt(var + LN_EPS)


def l2_normalize(z):
    return z * lax.rsqrt(jnp.sum(jnp.square(z), -1, keepdims=True) + RMS_EPS)


def causal_conv(x, buf, w, b=None):
    t_len = x.shape[1]
    xp = jnp.concatenate([buf, x], axis=1)
    y = sum(xp[:, j:j + t_len] * w[j] for j in range(CONV_W))
    if b is not None:
        y = y + b
    return y, xp[:, t_len:]


def rotary(z, pos):
    half = z.shape[-1] // 2
    inv = ROPE_BASE ** (-jnp.arange(half, dtype=F32) / half)
    ang = pos.astype(F32)[:, None] * inv
    cos = jnp.cos(ang)[None, :, None, :]
    sin = jnp.sin(ang)[None, :, None, :]
    z1, z2 = z[..., :half], z[..., half:]
    return jnp.concatenate([z1 * cos - z2 * sin, z1 * sin + z2 * cos], axis=-1)


def gla_chunked(q, k, v, log_f, s0, chunk_max):
    bsz, t_len, n_h, _ = q.shape
    dv = v.shape[-1]
    blk = math.gcd(t_len, chunk_max)
    n_blk = t_len // blk

    def blocks(a):
        return a.reshape(bsz, n_blk, blk, n_h, a.shape[-1]).transpose(1, 0, 3, 2, 4)

    qc, kc, vc, fc = blocks(q), blocks(k), blocks(v), blocks(log_f)
    cum = jnp.cumsum(fc, axis=3)
    cum_last = cum[:, :, :, -1:, :]
    q_dec = qc * jnp.exp(cum)
    k_inv = kc * jnp.exp(-cum)
    k_end = kc * jnp.exp(cum_last - cum)
    causal = jnp.tril(jnp.ones((blk, blk), dtype=bool))
    scores = jnp.where(causal, jnp.einsum('nbhtk,nbhsk->nbhts', q_dec, k_inv), 0.0)
    o_intra = jnp.einsum('nbhts,nbhsv->nbhtv', scores, vc)

    def step(state, inp):
        qd, ke, vv, cl = inp
        o = jnp.einsum('bhtk,bhkv->bhtv', qd, state)
        state = jnp.exp(cl[:, :, 0, :])[..., None] * state + jnp.einsum('bhsk,bhsv->bhkv', ke, vv)
        return state, o

    s_fin, o_inter = lax.scan(step, s0, (q_dec, k_end, vc, cum_last))
    o = (o_intra + o_inter).transpose(1, 0, 3, 2, 4).reshape(bsz, t_len, n_h, dv)
    return o, s_fin


def gated_delta_chunked(q, k, v, log_g, beta, s0, chunk_max):
    bsz, t_len, n_h, _ = q.shape
    dv = v.shape[-1]
    blk = math.gcd(t_len, chunk_max)
    n_blk = t_len // blk

    def blocks(a):
        return a.reshape(bsz, n_blk, blk, n_h, a.shape[-1]).transpose(1, 0, 3, 2, 4)

    def blocks_h(a):
        return a.reshape(bsz, n_blk, blk, n_h).transpose(1, 0, 3, 2)

    qc, kc, vc = blocks(q), blocks(k), blocks(v)
    gc, bc = blocks_h(log_g), blocks_h(beta)
    cum = jnp.cumsum(gc, axis=-1)
    incl = jnp.tril(jnp.ones((blk, blk), dtype=bool))
    strict = jnp.tril(jnp.ones((blk, blk), dtype=bool), k=-1)
    diff = cum[..., :, None] - cum[..., None, :]
    decay = jnp.where(incl, jnp.exp(jnp.where(incl, diff, 0.0)), 0.0)
    kk = jnp.einsum('nbhtk,nbhsk->nbhts', kc, kc)
    tri = jnp.where(strict, bc[..., :, None] * kk * decay, 0.0) + jnp.eye(blk, dtype=F32)
    w_v = lax.linalg.triangular_solve(tri, bc[..., None] * vc, left_side=True, lower=True, unit_diagonal=True)
    w_k = lax.linalg.triangular_solve(tri, (bc * jnp.exp(cum))[..., None] * kc, left_side=True, lower=True, unit_diagonal=True)
    qk = jnp.einsum('nbhtk,nbhsk->nbhts', qc, kc) * decay
    q_dec = qc * jnp.exp(cum)[..., None]
    k_end = kc * jnp.exp(cum[..., -1:] - cum)[..., None]
    g_end = jnp.exp(cum[..., -1])

    def step(state, inp):
        wv, wk, qkb, qd, ke, ge = inp
        u = wv - jnp.einsum('bhtk,bhkv->bhtv', wk, state)
        o = jnp.einsum('bhtk,bhkv->bhtv', qd, state) + jnp.einsum('bhts,bhsv->bhtv', qkb, u)
        state = ge[..., None, None] * state + jnp.einsum('bhsk,bhsv->bhkv', ke, u)
        return state, o

    s_fin, o = lax.scan(step, s0, (w_v, w_k, qk, q_dec, k_end, g_end))
    return o.transpose(1, 0, 3, 2, 4).reshape(bsz, t_len, n_h, dv), s_fin


def lru_combine(left, right):
    a_l, b_l = left
    a_r, b_r = right
    return a_l * a_r, a_r * b_l + b_r


def mixer_ab(h, layer, s_hgrn, s_delta, s_dconv, p):
    bsz, t_len, _ = h.shape
    u = (h @ p['w_in0']).astype(F32)
    qa, fa, ia, ga, qkv_b, zb, ab, bb = split_cols(u, IN0_SIZES)
    lb_all = jnp.cumsum(jax.nn.softmax(p['hgrn_lb_logits'].astype(F32), axis=0), axis=0)
    lb = lb_all[layer].reshape(H_A, DK_A)
    f = lb + (1.0 - lb) * jax.nn.sigmoid(fa.reshape(bsz, t_len, H_A, DK_A))
    q_a = jax.nn.silu(qa.reshape(bsz, t_len, H_A, DK_A))
    o_a, hgrn_new = gla_chunked(q_a, 1.0 - f, ia.reshape(bsz, t_len, H_A, DV_A), jnp.log(f), s_hgrn.astype(F32), CHUNK_A)
    o_a = rms_norm(o_a, p['hgrn_norm_w'].reshape(H_A, DV_A)) * jax.nn.silu(ga.reshape(bsz, t_len, H_A, DV_A))
    qkv, dconv_new = causal_conv(qkv_b, s_dconv.astype(F32), p['delta_conv_w'].astype(F32))
    qkv = jax.nn.silu(qkv)
    qb, kb, vb = split_cols(qkv, (H_B * DK_B, H_B * DK_B, H_B * DV_B))
    q_b = l2_normalize(qb.reshape(bsz, t_len, H_B, DK_B)) * (DK_B ** -0.5)
    k_b = l2_normalize(kb.reshape(bsz, t_len, H_B, DK_B))
    log_g = -jnp.exp(p['delta_a_log'].astype(F32)) * jax.nn.softplus(ab + p['delta_dt_bias'].astype(F32))
    beta = jax.nn.sigmoid(bb)
    o_b, delta_new = gated_delta_chunked(q_b, k_b, vb.reshape(bsz, t_len, H_B, DV_B), log_g, beta, s_delta.astype(F32), CHUNK_B)
    o_b = rms_norm(o_b, p['delta_norm_w']) * jax.nn.silu(zb.reshape(bsz, t_len, H_B, DV_B))
    mixed = jnp.concatenate([o_a.reshape(bsz, t_len, -1), o_b.reshape(bsz, t_len, -1)], axis=-1)
    return mixed.astype(h.dtype) @ p['w_out0'], (hgrn_new, delta_new, dconv_new)


def mixer_cd(h, start, s_rg, s_rgconv, s_ret, p):
    bsz, t_len, _ = h.shape
    u = (h @ p['w_in1']).astype(F32)
    rx, rgate, qd, kd, vd, gd = split_cols(u, IN1_SIZES)
    pos = start + jnp.arange(t_len, dtype=jnp.int32)
    xr, rgconv_new = causal_conv(rx, s_rgconv.astype(F32), p['rg_conv_w'].astype(F32), p['rg_conv_b'].astype(F32))
    xb = xr.reshape(bsz, t_len, RG_BLOCKS, RG_BW)
    r = jax.nn.sigmoid(jnp.einsum('btnc,ncd->btnd', xb, p['rg_wa'].astype(F32)).reshape(bsz, t_len, RG_WIDTH) + p['rg_ba'].astype(F32))
    i = jax.nn.sigmoid(jnp.einsum('btnc,ncd->btnd', xb, p['rg_wx'].astype(F32)).reshape(bsz, t_len, RG_WIDTH) + p['rg_bx'].astype(F32))
    log_a = -RG_C * r * jax.nn.softplus(-p['rg_lambda'].astype(F32))
    a = jnp.exp(log_a)
    mult = jnp.where((pos == 0)[None, :, None], 1.0, jnp.sqrt(-jnp.expm1(2.0 * log_a)))
    a_cum, b_cum = lax.associative_scan(lru_combine, (a, mult * i * xr), axis=1)
    hs = a_cum * s_rg.astype(F32)[:, None, :] + b_cum
    y_rg = hs * jax.nn.gelu(rgate)
    q_d = rotary(qd.reshape(bsz, t_len, H_D, DK_D), pos)
    k_d = rotary(kd.reshape(bsz, t_len, H_D, DK_D), pos) * (DK_D ** -0.5)
    log_gamma = jnp.log1p(-jnp.exp2(-5.0 - jnp.arange(H_D, dtype=F32)))
    log_f = jnp.broadcast_to(log_gamma[:, None], q_d.shape)
    o_d, ret_new = gla_chunked(q_d, k_d, vd.reshape(bsz, t_len, H_D, DV_D), log_f, s_ret.astype(F32), CHUNK_D)
    o_d = group_norm(o_d) * jax.nn.silu(gd.reshape(bsz, t_len, H_D, DV_D))
    mixed = jnp.concatenate([y_rg, o_d.reshape(bsz, t_len, -1)], axis=-1)
    return mixed.astype(h.dtype) @ p['w_out1'], (hs[:, -1], rgconv_new, ret_new)


def swiglu(h, w_in, w_out):
    gate, up = jnp.split(h @ w_in, 2, axis=-1)
    return (jax.nn.silu(gate) * up) @ w_out


def modulate(x, mod_j):
    return (x.astype(F32) * (1.0 + mod_j[:, 1, None, :]) + mod_j[:, 0, None, :]).astype(x.dtype)


def post_norm(x, out, mod_j, g, b, rho):
    z = ALPHA * x.astype(F32) + rho * mod_j[:, 2, None, :] * out.astype(F32)
    return layer_norm(z, g, b).astype(x.dtype)


def trunk(x, c, start, states, p):
    s_hgrn, s_delta, s_dconv, s_rg, s_rgconv, s_ret = states
    bsz = x.shape[0]
    c_act = jax.nn.silu(c.astype(F32))
    for layer in range(DEPTH):
        mod = (c_act @ p['ada_w'][layer].astype(F32) + p['ada_b'][layer].astype(F32)).reshape(bsz, 3, 3, D_MODEL)
        lg, lb = p['ln_g'][layer], p['ln_b'][layer]
        ffn = swiglu(modulate(x, mod[:, 0]), p['ffn_w_in'][layer, 0], p['ffn_w_out'][layer, 0])
        x = post_norm(x, ffn, mod[:, 0], lg[0], lb[0], 0.5)
        m_in = modulate(x, mod[:, 1])
        if layer % 2 == 0:
            mix, (s_hgrn, s_delta, s_dconv) = mixer_ab(m_in, layer, s_hgrn, s_delta, s_dconv, p)
        else:
            mix, (s_rg, s_rgconv, s_ret) = mixer_cd(m_in, start, s_rg, s_rgconv, s_ret, p)
        x = post_norm(x, mix, mod[:, 1], lg[1], lb[1], 1.0)
        ffn = swiglu(modulate(x, mod[:, 2]), p['ffn_w_in'][layer, 1], p['ffn_w_out'][layer, 1])
        x = post_norm(x, ffn, mod[:, 2], lg[2], lb[2], 0.5)
    return x, (s_hgrn, s_delta, s_dconv, s_rg, s_rgconv, s_ret)


def setup_inputs(seed: int = 0) -> dict:
    key = jax.random.key(seed)
    ks = jax.random.split(key, 40)

    def nrm(i, shape, scale):
        return jax.random.normal(ks[i], shape, F32) * scale

    u_lam = jax.random.uniform(ks[30], (RG_WIDTH,), F32, 0.9, 0.999)
    a_base = u_lam ** (1.0 / RG_C)
    rg_lambda = jnp.log(a_base) - jnp.log1p(-a_base)
    dt = jnp.exp(jax.random.uniform(ks[31], (H_B,), F32, math.log(1e-3), math.log(1e-1)))
    dt_bias = dt + jnp.log(-jnp.expm1(-dt))
    a_log = jnp.log(jax.random.uniform(ks[32], (H_B,), F32, 1.0, 16.0))
    return {
        'x_prompt': nrm(0, (BATCH, SEQ, D_MODEL), 1.0),
        'x_sample': nrm(1, (DEC_BATCH, DEC_SEQ, D_MODEL), 1.0),
        'c_prompt': nrm(2, (BATCH, D_MODEL), 1.0),
        'c_sample': nrm(3, (DEC_BATCH, D_MODEL), 1.0),
        'state_hgrn': nrm(4, (DEC_BATCH, H_A, DK_A, DV_A), 0.5),
        'state_delta': nrm(5, (DEC_BATCH, H_B, DK_B, DV_B), 0.5),
        'state_delta_conv': nrm(6, (DEC_BATCH, CONV_W - 1, C_B), 1.0),
        'state_rglru': nrm(7, (DEC_BATCH, RG_WIDTH), 0.5),
        'state_rglru_conv': nrm(8, (DEC_BATCH, CONV_W - 1, RG_WIDTH), 1.0),
        'state_ret': nrm(9, (DEC_BATCH, H_D, DK_D, DV_D), 0.5),
        'ada_w': nrm(10, (DEPTH, D_MODEL, 9 * D_MODEL), 0.5 * D_MODEL ** -0.5),
        'ada_b': nrm(11, (DEPTH, 9 * D_MODEL), 0.01),
        'ln_g': 1.0 + nrm(12, (DEPTH, 3, D_MODEL), 0.01),
        'ln_b': nrm(13, (DEPTH, 3, D_MODEL), 0.01),
        'ffn_w_in': nrm(14, (DEPTH, 2, D_MODEL, 2 * D_FF), D_MODEL ** -0.5),
        'ffn_w_out': nrm(15, (DEPTH, 2, D_FF, D_MODEL), BETA_INIT * D_FF ** -0.5),
        'w_in0': nrm(16, (D_MODEL, IN0), D_MODEL ** -0.5),
        'w_out0': nrm(17, (OUT0, D_MODEL), BETA_INIT * OUT0 ** -0.5),
        'hgrn_lb_logits': nrm(18, (DEPTH + 1, H_A * DK_A), 0.1),
        'hgrn_norm_w': 1.0 + nrm(19, (H_A * DV_A,), 0.01),
        'delta_conv_w': nrm(20, (CONV_W, C_B), CONV_W ** -0.5),
        'delta_a_log': a_log,
        'delta_dt_bias': dt_bias,
        'delta_norm_w': 1.0 + nrm(21, (DV_B,), 0.01),
        'w_in1': nrm(22, (D_MODEL, IN1), D_MODEL ** -0.5),
        'w_out1': nrm(23, (OUT1, D_MODEL), BETA_INIT * OUT1 ** -0.5),
        'rg_conv_w': nrm(24, (CONV_W, RG_WIDTH), CONV_W ** -0.5),
        'rg_conv_b': nrm(25, (RG_WIDTH,), 0.01),
        'rg_wa': nrm(26, (RG_BLOCKS, RG_BW, RG_BW), RG_BW ** -0.5),
        'rg_ba': nrm(27, (RG_WIDTH,), 0.01),
        'rg_wx': nrm(28, (RG_BLOCKS, RG_BW, RG_BW), RG_BW ** -0.5),
        'rg_bx': nrm(29, (RG_WIDTH,), 0.01),
        'rg_lambda': rg_lambda,
    }


def reference(x_prompt, x_sample, c_prompt, c_sample, state_hgrn, state_delta, state_delta_conv, state_rglru,
              state_rglru_conv, state_ret, ada_w, ada_b, ln_g, ln_b, ffn_w_in, ffn_w_out, w_in0, w_out0,
              hgrn_lb_logits, hgrn_norm_w, delta_conv_w, delta_a_log, delta_dt_bias, delta_norm_w, w_in1, w_out1,
              rg_conv_w, rg_conv_b, rg_wa, rg_ba, rg_wx, rg_bx, rg_lambda):
    p = {
        'ada_w': ada_w, 'ada_b': ada_b, 'ln_g': ln_g, 'ln_b': ln_b,
        'ffn_w_in': ffn_w_in, 'ffn_w_out': ffn_w_out,
        'w_in0': w_in0, 'w_out0': w_out0, 'hgrn_lb_logits': hgrn_lb_logits, 'hgrn_norm_w': hgrn_norm_w,
        'delta_conv_w': delta_conv_w, 'delta_a_log': delta_a_log, 'delta_dt_bias': delta_dt_bias,
        'delta_norm_w': delta_norm_w, 'w_in1': w_in1, 'w_out1': w_out1,
        'rg_conv_w': rg_conv_w, 'rg_conv_b': rg_conv_b, 'rg_wa': rg_wa, 'rg_ba': rg_ba,
        'rg_wx': rg_wx, 'rg_bx': rg_bx, 'rg_lambda': rg_lambda,
    }
    nb = x_prompt.shape[0]
    prompt_states = (
        jnp.zeros((nb, H_A, DK_A, DV_A), F32),
        jnp.zeros((nb, H_B, DK_B, DV_B), F32),
        jnp.zeros((nb, CONV_W - 1, C_B), F32),
        jnp.zeros((nb, RG_WIDTH), F32),
        jnp.zeros((nb, CONV_W - 1, RG_WIDTH), F32),
        jnp.zeros((nb, H_D, DK_D, DV_D), F32),
    )
    y_prompt, (p_hgrn, p_delta, p_dconv, p_rg, p_rgconv, p_ret) = trunk(x_prompt, c_prompt, 0, prompt_states, p)
    sample_states = (state_hgrn, state_delta, state_delta_conv, state_rglru, state_rglru_conv, state_ret)
    y_sample, (s_hgrn, s_delta, s_dconv, s_rg, s_rgconv, s_ret) = trunk(x_sample, c_sample, PAST_LEN, sample_states, p)
    return (y_prompt, y_sample, p_hgrn, p_delta, p_dconv, p_rg, p_rgconv, p_ret,
            s_hgrn, s_delta, s_dconv, s_rg, s_rgconv, s_ret)
```

```python
import functools
import math

import jax
import jax.numpy as jnp
import numpy as np
from jax import lax
from jax.experimental import pallas as pl
from jax.experimental.pallas import tpu as pltpu

D_MODEL = 4096
DEPTH = 2
PAST_LEN = 16384
MIX_HALF = D_MODEL // 2
DK_A = 128
DV_A = 128
H_A = MIX_HALF // DV_A
DK_B = 128
DV_B = 128
H_B = MIX_HALF // DV_B
CONV_W = 4
C_B = 2 * H_B * DK_B + H_B * DV_B
RG_WIDTH = MIX_HALF
RG_BW = 128
RG_BLOCKS = RG_WIDTH // RG_BW
RG_C = 8.0
DK_D = 128
DV_D = 256
H_D = MIX_HALF // DV_D
D_FF = 11008
CHUNK_A = 16
CHUNK_B = 64
CHUNK_D = 64
ROPE_BASE = 10000.0
LN_EPS = 1e-5
RMS_EPS = 1e-6
ALPHA = (2 * DEPTH) ** 0.25

IN0_SIZES = (H_A * DK_A, H_A * DK_A, H_A * DV_A, H_A * DV_A, C_B, H_B * DV_B, H_B, H_B)
IN0 = sum(IN0_SIZES)
IN0_MAIN = IN0 - 2 * H_B
IN1_SIZES = (RG_WIDTH, RG_WIDTH, H_D * DK_D, H_D * DK_D, H_D * DV_D, H_D * DV_D)
IN1 = sum(IN1_SIZES)

F32 = jnp.float32
BF16 = jnp.bfloat16

V7X_VMEM_LIMIT_BYTES = 56 * 1024 * 1024
V7X_LANES = 128
FFN_TN = 256


def _cparams(n_axes):
    return pltpu.CompilerParams(dimension_semantics=("arbitrary",) * n_axes,
                                vmem_limit_bytes=V7X_VMEM_LIMIT_BYTES)


def _mm_kernel(x_ref, w_ref, o_ref):
    o_ref[...] = jnp.dot(x_ref[...], w_ref[...], preferred_element_type=F32).astype(o_ref.dtype)


def _matmul(x, w, n_cols, out_dtype, tm, tn):
    m, k = x.shape
    assert m % tm == 0 and n_cols % tn == 0 and w.shape[0] == k
    return pl.pallas_call(
        _mm_kernel,
        grid=(m // tm, n_cols // tn),
        in_specs=[pl.BlockSpec((tm, k), lambda i, j: (i, 0)),
                  pl.BlockSpec((k, tn), lambda i, j: (0, j))],
        out_specs=pl.BlockSpec((tm, tn), lambda i, j: (i, j)),
        out_shape=jax.ShapeDtypeStruct((m, n_cols), out_dtype),
        compiler_params=_cparams(2),
        name="matmul",
    )(x, w)


def _swiglu_kernel(x_ref, wg_ref, wu_ref, o_ref):
    x = x_ref[...]
    gate = jnp.dot(x, wg_ref[...], preferred_element_type=F32)
    up = jnp.dot(x, wu_ref[...], preferred_element_type=F32)
    o_ref[...] = (gate * jax.nn.sigmoid(gate) * up).astype(o_ref.dtype)


def _ffn_in(x, w_in, tm):
    m, k = x.shape
    n_tiles = D_FF // FFN_TN
    return pl.pallas_call(
        _swiglu_kernel,
        grid=(m // tm, n_tiles),
        in_specs=[pl.BlockSpec((tm, k), lambda i, j: (i, 0)),
                  pl.BlockSpec((k, FFN_TN), lambda i, j: (0, j)),
                  pl.BlockSpec((k, FFN_TN), lambda i, j: (0, j + n_tiles))],
        out_specs=pl.BlockSpec((tm, FFN_TN), lambda i, j: (i, j)),
        out_shape=jax.ShapeDtypeStruct((m, D_FF), BF16),
        compiler_params=_cparams(2),
        name="ffn_in",
    )(x, w_in, w_in)


def _ada_kernel(c_ref, w_ref, b_ref, o_ref):
    o_ref[...] = jnp.dot(c_ref[...], w_ref[...].astype(BF16), preferred_element_type=F32) + b_ref[...]


def _ada(c_act, ada_w, ada_b, tn=512):
    r = c_act.shape[0]
    n = ada_w.shape[2]
    return pl.pallas_call(
        _ada_kernel,
        grid=(DEPTH, n // tn),
        in_specs=[pl.BlockSpec((r, D_MODEL), lambda l, j: (0, 0)),
                  pl.BlockSpec((None, D_MODEL, tn), lambda l, j: (l, 0, j)),
                  pl.BlockSpec((None, 1, tn), lambda l, j: (l, 0, j))],
        out_specs=pl.BlockSpec((None, r, tn), lambda l, j: (l, 0, j)),
        out_shape=jax.ShapeDtypeStruct((DEPTH, r, n), F32),
        compiler_params=_cparams(2),
        name="ada",
    )(c_act, ada_w, ada_b.reshape(DEPTH, 1, n))


def _modulate_kernel(x_ref, shift_ref, scale_ref, h_ref):
    h_ref[...] = (x_ref[...] * (1.0 + scale_ref[...]) + shift_ref[...]).astype(h_ref.dtype)


def _post_norm_body(x_ref, out_ref, gate_ref, g_ref, b_ref, rho):
    z = ALPHA * x_ref[...] + rho * gate_ref[...] * out_ref[...]
    mu = jnp.mean(z, axis=-1, keepdims=True)
    zc = z - mu
    var = jnp.mean(zc * zc, axis=-1, keepdims=True)
    return zc * lax.rsqrt(var + LN_EPS) * g_ref[...] + b_ref[...]


def _post_norm_mod_kernel(x_ref, out_ref, gate_ref, g_ref, b_ref, shift_ref, scale_ref, xn_ref, h_ref, *, rho):
    xn = _post_norm_body(x_ref, out_ref, gate_ref, g_ref, b_ref, rho)
    xn_ref[...] = xn
    h_ref[...] = (xn * (1.0 + scale_ref[...]) + shift_ref[...]).astype(h_ref.dtype)


def _post_norm_kernel(x_ref, out_ref, gate_ref, g_ref, b_ref, xn_ref, *, rho):
    xn_ref[...] = _post_norm_body(x_ref, out_ref, gate_ref, g_ref, b_ref, rho)


class _RowLayout:
    def __init__(self, bsz, t_len):
        self.bsz, self.t_len = bsz, t_len
        self.m = bsz * t_len
        self.per_tile = t_len >= 256
        self.tm = 128

    def mod_arg(self, v):
        if self.per_tile:
            return v.reshape(self.bsz, 1, D_MODEL)
        return jnp.repeat(v, self.t_len, axis=0)

    def mod_spec(self):
        if self.per_tile:
            per = self.t_len // self.tm
            return pl.BlockSpec((None, 1, D_MODEL), lambda i: (i // per, 0, 0))
        return pl.BlockSpec((self.tm, D_MODEL), lambda i: (i, 0))

    def row_spec(self):
        return pl.BlockSpec((self.tm, D_MODEL), lambda i: (i, 0))

    @staticmethod
    def vec_spec():
        return pl.BlockSpec((1, D_MODEL), lambda i: (0, 0))


def _modulate(lay, x, shift, scale):
    return pl.pallas_call(
        _modulate_kernel,
        grid=(lay.m // lay.tm,),
        in_specs=[lay.row_spec(), lay.mod_spec(), lay.mod_spec()],
        out_specs=lay.row_spec(),
        out_shape=jax.ShapeDtypeStruct((lay.m, D_MODEL), BF16),
        compiler_params=_cparams(1),
        name="modulate",
    )(x, lay.mod_arg(shift), lay.mod_arg(scale))


def _post_norm(lay, x, out, gate, g, b, rho, nxt):
    g2, b2 = g.reshape(1, D_MODEL), b.reshape(1, D_MODEL)
    common = dict(grid=(lay.m // lay.tm,), compiler_params=_cparams(1))
    base_specs = [lay.row_spec(), lay.row_spec(), lay.mod_spec(), lay.vec_spec(), lay.vec_spec()]
    if nxt is None:
        return pl.pallas_call(
            functools.partial(_post_norm_kernel, rho=rho),
            in_specs=base_specs,
            out_specs=lay.row_spec(),
            out_shape=jax.ShapeDtypeStruct((lay.m, D_MODEL), F32),
            name="post_norm", **common,
        )(x, out, lay.mod_arg(gate), g2, b2), None
    shift, scale = nxt
    return pl.pallas_call(
        functools.partial(_post_norm_mod_kernel, rho=rho),
        in_specs=base_specs + [lay.mod_spec(), lay.mod_spec()],
        out_specs=[lay.row_spec(), lay.row_spec()],
        out_shape=[jax.ShapeDtypeStruct((lay.m, D_MODEL), F32), jax.ShapeDtypeStruct((lay.m, D_MODEL), BF16)],
        name="post_norm_mod", **common,
    )(x, out, lay.mod_arg(gate), g2, b2, lay.mod_arg(shift), lay.mod_arg(scale))


def _split_cols(u, sizes):
    return jnp.split(u, [int(s) for s in np.cumsum(sizes)[:-1]], axis=-1)


def _rms_norm(z, g):
    return z * lax.rsqrt(jnp.mean(jnp.square(z), -1, keepdims=True) + RMS_EPS) * g.astype(F32)


def _group_norm(z):
    mu = jnp.mean(z, -1, keepdims=True)
    var = jnp.mean(jnp.square(z - mu), -1, keepdims=True)
    return (z - mu) * lax.rsqrt(var + LN_EPS)


def _l2_normalize(z):
    return z * lax.rsqrt(jnp.sum(jnp.square(z), -1, keepdims=True) + RMS_EPS)


def _causal_conv(x, buf, w, b=None):
    t_len = x.shape[1]
    xp = jnp.concatenate([buf, x], axis=1)
    y = sum(xp[:, j:j + t_len] * w[j] for j in range(CONV_W))
    if b is not None:
        y = y + b
    return y, xp[:, t_len:]


def _rotary(z, pos):
    half = z.shape[-1] // 2
    inv = ROPE_BASE ** (-jnp.arange(half, dtype=F32) / half)
    ang = pos.astype(F32)[:, None] * inv
    cos = jnp.cos(ang)[None, :, None, :]
    sin = jnp.sin(ang)[None, :, None, :]
    z1, z2 = z[..., :half], z[..., half:]
    return jnp.concatenate([z1 * cos - z2 * sin, z1 * sin + z2 * cos], axis=-1)


def _gla_chunked(q, k, v, log_f, s0, chunk_max):
    bsz, t_len, n_h, _ = q.shape
    dv = v.shape[-1]
    blk = math.gcd(t_len, chunk_max)
    n_blk = t_len // blk

    def blocks(a):
        return a.reshape(bsz, n_blk, blk, n_h, a.shape[-1]).transpose(1, 0, 3, 2, 4)

    qc, kc, vc, fc = blocks(q), blocks(k), blocks(v), blocks(log_f)
    cum = jnp.cumsum(fc, axis=3)
    cum_last = cum[:, :, :, -1:, :]
    q_dec = qc * jnp.exp(cum)
    k_inv = kc * jnp.exp(-cum)
    k_end = kc * jnp.exp(cum_last - cum)
    causal = jnp.tril(jnp.ones((blk, blk), dtype=bool))
    scores = jnp.where(causal, jnp.einsum('nbhtk,nbhsk->nbhts', q_dec, k_inv), 0.0)
    o_intra = jnp.einsum('nbhts,nbhsv->nbhtv', scores, vc)

    def step(state, inp):
        qd, ke, vv, cl = inp
        o = jnp.einsum('bhtk,bhkv->bhtv', qd, state)
        state = jnp.exp(cl[:, :, 0, :])[..., None] * state + jnp.einsum('bhsk,bhsv->bhkv', ke, vv)
        return state, o

    s_fin, o_inter = lax.scan(step, s0, (q_dec, k_end, vc, cum_last))
    o = (o_intra + o_inter).transpose(1, 0, 3, 2, 4).reshape(bsz, t_len, n_h, dv)
    return o, s_fin


def _gated_delta_chunked(q, k, v, log_g, beta, s0, chunk_max):
    bsz, t_len, n_h, _ = q.shape
    dv = v.shape[-1]
    blk = math.gcd(t_len, chunk_max)
    n_blk = t_len // blk

    def blocks(a):
        return a.reshape(bsz, n_blk, blk, n_h, a.shape[-1]).transpose(1, 0, 3, 2, 4)

    def blocks_h(a):
        return a.reshape(bsz, n_blk, blk, n_h).transpose(1, 0, 3, 2)

    qc, kc, vc = blocks(q), blocks(k), blocks(v)
    gc, bc = blocks_h(log_g), blocks_h(beta)
    cum = jnp.cumsum(gc, axis=-1)
    incl = jnp.tril(jnp.ones((blk, blk), dtype=bool))
    strict = jnp.tril(jnp.ones((blk, blk), dtype=bool), k=-1)
    diff = cum[..., :, None] - cum[..., None, :]
    decay = jnp.where(incl, jnp.exp(jnp.where(incl, diff, 0.0)), 0.0)
    kk = jnp.einsum('nbhtk,nbhsk->nbhts', kc, kc)
    tri = jnp.where(strict, bc[..., :, None] * kk * decay, 0.0) + jnp.eye(blk, dtype=F32)
    w_v = lax.linalg.triangular_solve(tri, bc[..., None] * vc, left_side=True, lower=True, unit_diagonal=True)
    w_k = lax.linalg.triangular_solve(tri, (bc * jnp.exp(cum))[..., None] * kc, left_side=True, lower=True,
                                      unit_diagonal=True)
    qk = jnp.einsum('nbhtk,nbhsk->nbhts', qc, kc) * decay
    q_dec = qc * jnp.exp(cum)[..., None]
    k_end = kc * jnp.exp(cum[..., -1:] - cum)[..., None]
    g_end = jnp.exp(cum[..., -1])

    def step(state, inp):
        wv, wk, qkb, qd, ke, ge = inp
        u = wv - jnp.einsum('bhtk,bhkv->bhtv', wk, state)
        o = jnp.einsum('bhtk,bhkv->bhtv', qd, state) + jnp.einsum('bhts,bhsv->bhtv', qkb, u)
        state = ge[..., None, None] * state + jnp.einsum('bhsk,bhsv->bhkv', ke, u)
        return state, o

    s_fin, o = lax.scan(step, s0, (w_v, w_k, qk, q_dec, k_end, g_end))
    return o.transpose(1, 0, 3, 2, 4).reshape(bsz, t_len, n_h, dv), s_fin


def _lru_combine(left, right):
    a_l, b_l = left
    a_r, b_r = right
    return a_l * a_r, a_r * b_l + b_r


def _mixer_ab(u, layer, s_hgrn, s_delta, s_dconv, p):
    bsz, t_len, _ = u.shape
    qa, fa, ia, ga, qkv_b, zb, ab, bb = _split_cols(u, IN0_SIZES)
    lb_all = jnp.cumsum(jax.nn.softmax(p['hgrn_lb_logits'].astype(F32), axis=0), axis=0)
    lb = lb_all[layer].reshape(H_A, DK_A)
    f = lb + (1.0 - lb) * jax.nn.sigmoid(fa.reshape(bsz, t_len, H_A, DK_A))
    q_a = jax.nn.silu(qa.reshape(bsz, t_len, H_A, DK_A))
    o_a, hgrn_new = _gla_chunked(q_a, 1.0 - f, ia.reshape(bsz, t_len, H_A, DV_A), jnp.log(f), s_hgrn.astype(F32),
                                 CHUNK_A)
    o_a = _rms_norm(o_a, p['hgrn_norm_w'].reshape(H_A, DV_A)) * jax.nn.silu(ga.reshape(bsz, t_len, H_A, DV_A))
    qkv, dconv_new = _causal_conv(qkv_b, s_dconv.astype(F32), p['delta_conv_w'].astype(F32))
    qkv = jax.nn.silu(qkv)
    qb, kb, vb = _split_cols(qkv, (H_B * DK_B, H_B * DK_B, H_B * DV_B))
    q_b = _l2_normalize(qb.reshape(bsz, t_len, H_B, DK_B)) * (DK_B ** -0.5)
    k_b = _l2_normalize(kb.reshape(bsz, t_len, H_B, DK_B))
    log_g = -jnp.exp(p['delta_a_log'].astype(F32)) * jax.nn.softplus(ab + p['delta_dt_bias'].astype(F32))
    beta = jax.nn.sigmoid(bb)
    o_b, delta_new = _gated_delta_chunked(q_b, k_b, vb.reshape(bsz, t_len, H_B, DV_B), log_g, beta,
                                          s_delta.astype(F32), CHUNK_B)
    o_b = _rms_norm(o_b, p['delta_norm_w']) * jax.nn.silu(zb.reshape(bsz, t_len, H_B, DV_B))
    mixed = jnp.concatenate([o_a.reshape(bsz, t_len, -1), o_b.reshape(bsz, t_len, -1)], axis=-1)
    return mixed, (hgrn_new, delta_new, dconv_new)


def _mixer_cd(u, start, s_rg, s_rgconv, s_ret, p):
    bsz, t_len, _ = u.shape
    rx, rgate, qd, kd, vd, gd = _split_cols(u, IN1_SIZES)
    pos = start + jnp.arange(t_len, dtype=jnp.int32)
    xr, rgconv_new = _causal_conv(rx, s_rgconv.astype(F32), p['rg_conv_w'].astype(F32), p['rg_conv_b'].astype(F32))
    xb = xr.reshape(bsz, t_len, RG_BLOCKS, RG_BW)
    r = jax.nn.sigmoid(jnp.einsum('btnc,ncd->btnd', xb, p['rg_wa'].astype(F32)).reshape(bsz, t_len, RG_WIDTH)
                       + p['rg_ba'].astype(F32))
    i = jax.nn.sigmoid(jnp.einsum('btnc,ncd->btnd', xb, p['rg_wx'].astype(F32)).reshape(bsz, t_len, RG_WIDTH)
                       + p['rg_bx'].astype(F32))
    log_a = -RG_C * r * jax.nn.softplus(-p['rg_lambda'].astype(F32))
    a = jnp.exp(log_a)
    mult = jnp.where((pos == 0)[None, :, None], 1.0, jnp.sqrt(-jnp.expm1(2.0 * log_a)))
    a_cum, b_cum = lax.associative_scan(_lru_combine, (a, mult * i * xr), axis=1)
    hs = a_cum * s_rg.astype(F32)[:, None, :] + b_cum
    y_rg = hs * jax.nn.gelu(rgate)
    q_d = _rotary(qd.reshape(bsz, t_len, H_D, DK_D), pos)
    k_d = _rotary(kd.reshape(bsz, t_len, H_D, DK_D), pos) * (DK_D ** -0.5)
    log_gamma = jnp.log1p(-jnp.exp2(-5.0 - jnp.arange(H_D, dtype=F32)))
    log_f = jnp.broadcast_to(log_gamma[:, None], q_d.shape)
    o_d, ret_new = _gla_chunked(q_d, k_d, vd.reshape(bsz, t_len, H_D, DV_D), log_f, s_ret.astype(F32), CHUNK_D)
    o_d = _group_norm(o_d) * jax.nn.silu(gd.reshape(bsz, t_len, H_D, DV_D))
    mixed = jnp.concatenate([y_rg, o_d.reshape(bsz, t_len, -1)], axis=-1)
    return mixed, (hs[:, -1], rgconv_new, ret_new)


def _trunk(x, mod_all, start, states, p, wb):
    s_hgrn, s_delta, s_dconv, s_rg, s_rgconv, s_ret = states
    bsz, t_len, _ = x.shape
    lay = _RowLayout(bsz, t_len)
    m = lay.m
    tm_mm = min(m, 1024)
    tm_ffn = min(m, 2048)
    tm_out = 512
    x = x.reshape(m, D_MODEL)

    def mod_of(layer, j, which):
        return mod_all[layer].reshape(bsz, 3, 3, D_MODEL)[:, j, which]

    h = _modulate(lay, x, mod_of(0, 0, 0), mod_of(0, 0, 1))
    for layer in range(DEPTH):
        lg, lb = p['ln_g'][layer], p['ln_b'][layer]
        mid = _ffn_in(h, wb['ffn_w_in'][layer][0], tm_ffn)
        ffn = _matmul(mid, wb['ffn_w_out'][layer][0], D_MODEL, F32, tm_out, 512)
        x, h = _post_norm(lay, x, ffn, mod_of(layer, 0, 2), lg[0], lb[0], 0.5,
                          (mod_of(layer, 1, 0), mod_of(layer, 1, 1)))
        if layer % 2 == 0:
            u_main = _matmul(h, wb['w_in0'], IN0_MAIN, F32, tm_mm, 1024)
            u_tail = _matmul(h, wb['w_in0_tail'], V7X_LANES, F32, tm_mm, V7X_LANES)[:, :2 * H_B]
            u = jnp.concatenate([u_main, u_tail], axis=-1).reshape(bsz, t_len, IN0)
            mixed, (s_hgrn, s_delta, s_dconv) = _mixer_ab(u, layer, s_hgrn, s_delta, s_dconv, p)
            w_o = wb['w_out0']
        else:
            u = _matmul(h, wb['w_in1'], IN1, F32, tm_mm, 1024).reshape(bsz, t_len, IN1)
            mixed, (s_rg, s_rgconv, s_ret) = _mixer_cd(u, start, s_rg, s_rgconv, s_ret, p)
            w_o = wb['w_out1']
        mix = _matmul(mixed.reshape(m, D_MODEL).astype(BF16), w_o, D_MODEL, F32, tm_mm, 1024)
        x, h = _post_norm(lay, x, mix, mod_of(layer, 1, 2), lg[1], lb[1], 1.0,
                          (mod_of(layer, 2, 0), mod_of(layer, 2, 1)))
        mid = _ffn_in(h, wb['ffn_w_in'][layer][1], tm_ffn)
        ffn = _matmul(mid, wb['ffn_w_out'][layer][1], D_MODEL, F32, tm_out, 512)
        nxt = (mod_of(layer + 1, 0, 0), mod_of(layer + 1, 0, 1)) if layer + 1 < DEPTH else None
        x, h = _post_norm(lay, x, ffn, mod_of(layer, 2, 2), lg[2], lb[2], 0.5, nxt)
    return x.reshape(bsz, t_len, D_MODEL), (s_hgrn, s_delta, s_dconv, s_rg, s_rgconv, s_ret)


def kernel(x_prompt, x_sample, c_prompt, c_sample, state_hgrn, state_delta, state_delta_conv, state_rglru, state_rglru_conv, state_ret, ada_w, ada_b, ln_g, ln_b, ffn_w_in, ffn_w_out, w_in0, w_out0, hgrn_lb_logits, hgrn_norm_w, delta_conv_w, delta_a_log, delta_dt_bias, delta_norm_w, w_in1, w_out1, rg_conv_w, rg_conv_b, rg_wa, rg_ba, rg_wx, rg_bx, rg_lambda):
    p = {
        'ln_g': ln_g, 'ln_b': ln_b, 'hgrn_lb_logits': hgrn_lb_logits, 'hgrn_norm_w': hgrn_norm_w,
        'delta_conv_w': delta_conv_w, 'delta_a_log': delta_a_log, 'delta_dt_bias': delta_dt_bias,
        'delta_norm_w': delta_norm_w, 'rg_conv_w': rg_conv_w, 'rg_conv_b': rg_conv_b, 'rg_wa': rg_wa,
        'rg_ba': rg_ba, 'rg_wx': rg_wx, 'rg_bx': rg_bx, 'rg_lambda': rg_lambda,
    }
    wb = {
        'ffn_w_in': [[ffn_w_in[l, s].astype(BF16) for s in range(2)] for l in range(DEPTH)],
        'ffn_w_out': [[ffn_w_out[l, s].astype(BF16) for s in range(2)] for l in range(DEPTH)],
        'w_in0': w_in0.astype(BF16),
        'w_in0_tail': jnp.pad(w_in0[:, IN0_MAIN:], ((0, 0), (0, V7X_LANES - 2 * H_B))).astype(BF16),
        'w_out0': w_out0.astype(BF16),
        'w_in1': w_in1.astype(BF16),
        'w_out1': w_out1.astype(BF16),
    }
    nb, ns = x_prompt.shape[0], x_sample.shape[0]
    c_all = jnp.concatenate([c_prompt, c_sample], axis=0).astype(F32)
    rows = nb + ns
    rows_pad = -(-rows // 16) * 16
    c_act = jnp.pad(jax.nn.silu(c_all), ((0, rows_pad - rows), (0, 0))).astype(BF16)
    mod = _ada(c_act, ada_w, ada_b)
    mod_prompt, mod_sample = mod[:, :nb], mod[:, nb:rows]

    prompt_states = (
        jnp.zeros((nb, H_A, DK_A, DV_A), F32),
        jnp.zeros((nb, H_B, DK_B, DV_B), F32),
        jnp.zeros((nb, CONV_W - 1, C_B), F32),
        jnp.zeros((nb, RG_WIDTH), F32),
        jnp.zeros((nb, CONV_W - 1, RG_WIDTH), F32),
        jnp.zeros((nb, H_D, DK_D, DV_D), F32),
    )
    y_prompt, ps = _trunk(x_prompt, mod_prompt, 0, prompt_states, p, wb)
    sample_states = (state_hgrn, state_delta, state_delta_conv, state_rglru, state_rglru_conv, state_ret)
    y_sample, ss = _trunk(x_sample, mod_sample, PAST_LEN, sample_states, p, wb)
    return (y_prompt, y_sample) + tuple(ps) + tuple(ss)
```

```python
import functools
import math

import jax
import jax.numpy as jnp
import numpy as np
from jax import lax
from jax.experimental import pallas as pl
from jax.experimental.pallas import tpu as pltpu

D_MODEL = 4096
DEPTH = 2
PAST_LEN = 16384
MIX_HALF = D_MODEL // 2
DK_A = 128
DV_A = 128
H_A = MIX_HALF // DV_A
DK_B = 128
DV_B = 128
H_B = MIX_HALF // DV_B
CONV_W = 4
C_B = 2 * H_B * DK_B + H_B * DV_B
RG_WIDTH = MIX_HALF
RG_BW = 128
RG_BLOCKS = RG_WIDTH // RG_BW
RG_C = 8.0
DK_D = 128
DV_D = 256
H_D = MIX_HALF // DV_D
D_FF = 11008
CHUNK_A = 16
CHUNK_B = 64
CHUNK_D = 64
ROPE_BASE = 10000.0
LN_EPS = 1e-5
RMS_EPS = 1e-6
ALPHA = (2 * DEPTH) ** 0.25

IN0_SIZES = (H_A * DK_A, H_A * DK_A, H_A * DV_A, H_A * DV_A, C_B, H_B * DV_B, H_B, H_B)
IN0 = sum(IN0_SIZES)
IN0_MAIN = IN0 - 2 * H_B
IN1_SIZES = (RG_WIDTH, RG_WIDTH, H_D * DK_D, H_D * DK_D, H_D * DV_D, H_D * DV_D)
IN1 = sum(IN1_SIZES)

F32 = jnp.float32
BF16 = jnp.bfloat16
HI = lax.Precision.HIGHEST

V7X_VMEM_LIMIT_BYTES = 56 * 1024 * 1024
V7X_LANES = 128
SUBLANES = 8
FFN_TN = 256
HEADS_PER_STEP = 4
GATE_BETA_LANE = 16


def _cparams(n_axes):
    return pltpu.CompilerParams(dimension_semantics=("arbitrary",) * n_axes,
                                vmem_limit_bytes=V7X_VMEM_LIMIT_BYTES)


def _mm_kernel(x_ref, w_ref, o_ref):
    o_ref[...] = jnp.dot(x_ref[...], w_ref[...], preferred_element_type=F32).astype(o_ref.dtype)


def _matmul(x, w, n_cols, out_dtype, tm, tn):
    m, k = x.shape
    assert m % tm == 0 and n_cols % tn == 0 and w.shape[0] == k
    return pl.pallas_call(
        _mm_kernel,
        grid=(m // tm, n_cols // tn),
        in_specs=[pl.BlockSpec((tm, k), lambda i, j: (i, 0)),
                  pl.BlockSpec((k, tn), lambda i, j: (0, j))],
        out_specs=pl.BlockSpec((tm, tn), lambda i, j: (i, j)),
        out_shape=jax.ShapeDtypeStruct((m, n_cols), out_dtype),
        compiler_params=_cparams(2),
        name="matmul",
    )(x, w)


def _mm2_kernel(xa_ref, xb_ref, wa_ref, wb_ref, o_ref):
    o_ref[...] = (jnp.dot(xa_ref[...], wa_ref[...], preferred_element_type=F32)
                  + jnp.dot(xb_ref[...], wb_ref[...], preferred_element_type=F32)).astype(o_ref.dtype)


def _matmul_halves(xa, xb, w, tm, tn):
    m, kh = xa.shape
    n = w.shape[1]
    return pl.pallas_call(
        _mm2_kernel,
        grid=(m // tm, n // tn),
        in_specs=[pl.BlockSpec((tm, kh), lambda i, j: (i, 0)), pl.BlockSpec((tm, kh), lambda i, j: (i, 0)),
                  pl.BlockSpec((kh, tn), lambda i, j: (0, j)), pl.BlockSpec((kh, tn), lambda i, j: (1, j))],
        out_specs=pl.BlockSpec((tm, tn), lambda i, j: (i, j)),
        out_shape=jax.ShapeDtypeStruct((m, n), F32),
        compiler_params=_cparams(2),
        name="matmul_halves",
    )(xa, xb, w, w)


def _swiglu_kernel(x_ref, wg_ref, wu_ref, o_ref):
    x = x_ref[...]
    gate = jnp.dot(x, wg_ref[...].astype(BF16), preferred_element_type=F32)
    up = jnp.dot(x, wu_ref[...].astype(BF16), preferred_element_type=F32)
    o_ref[...] = (gate * jax.nn.sigmoid(gate) * up).astype(o_ref.dtype)


def _ffn_in(x, ffn_w_in, layer, half, tm):
    m, k = x.shape
    n_tiles = D_FF // FFN_TN
    return pl.pallas_call(
        _swiglu_kernel,
        grid=(m // tm, n_tiles),
        in_specs=[pl.BlockSpec((tm, k), lambda i, j: (i, 0), pipeline_mode=pl.Buffered(1)),
                  pl.BlockSpec((None, None, k, FFN_TN), lambda i, j: (layer, half, 0, j)),
                  pl.BlockSpec((None, None, k, FFN_TN), lambda i, j: (layer, half, 0, j + n_tiles))],
        out_specs=pl.BlockSpec((tm, FFN_TN), lambda i, j: (i, j)),
        out_shape=jax.ShapeDtypeStruct((m, D_FF), BF16),
        compiler_params=_cparams(2),
        name="ffn_in",
    )(x, ffn_w_in, ffn_w_in)


def _ada_kernel(c_ref, w_ref, b_ref, o_ref):
    o_ref[...] = jnp.dot(c_ref[...], w_ref[...].astype(BF16), preferred_element_type=F32) + b_ref[...]


def _ada(c_act, ada_w, ada_b, tn=512):
    r = c_act.shape[0]
    n = ada_w.shape[2]
    return pl.pallas_call(
        _ada_kernel,
        grid=(DEPTH, n // tn),
        in_specs=[pl.BlockSpec((r, D_MODEL), lambda l, j: (0, 0)),
                  pl.BlockSpec((None, D_MODEL, tn), lambda l, j: (l, 0, j)),
                  pl.BlockSpec((None, 1, tn), lambda l, j: (l, 0, j))],
        out_specs=pl.BlockSpec((None, r, tn), lambda l, j: (l, 0, j)),
        out_shape=jax.ShapeDtypeStruct((DEPTH, r, n), F32),
        compiler_params=_cparams(2),
        name="ada",
    )(c_act, ada_w, ada_b.reshape(DEPTH, 1, n))


def _modulate_kernel(x_ref, shift_ref, scale_ref, h_ref):
    h_ref[...] = (x_ref[...] * (1.0 + scale_ref[...]) + shift_ref[...]).astype(h_ref.dtype)


def _post_norm_body(x_ref, out_ref, gate_ref, g_ref, b_ref, rho):
    z = ALPHA * x_ref[...] + rho * gate_ref[...] * out_ref[...]
    mu = jnp.mean(z, axis=-1, keepdims=True)
    zc = z - mu
    var = jnp.mean(zc * zc, axis=-1, keepdims=True)
    return zc * lax.rsqrt(var + LN_EPS) * g_ref[...] + b_ref[...]


def _post_norm_mod_kernel(x_ref, out_ref, gate_ref, g_ref, b_ref, shift_ref, scale_ref, xn_ref, h_ref, *, rho):
    xn = _post_norm_body(x_ref, out_ref, gate_ref, g_ref, b_ref, rho)
    xn_ref[...] = xn
    h_ref[...] = (xn * (1.0 + scale_ref[...]) + shift_ref[...]).astype(h_ref.dtype)


def _post_norm_kernel(x_ref, out_ref, gate_ref, g_ref, b_ref, xn_ref, *, rho):
    xn_ref[...] = _post_norm_body(x_ref, out_ref, gate_ref, g_ref, b_ref, rho)


class _RowLayout:
    def __init__(self, bsz, t_len):
        self.bsz, self.t_len = bsz, t_len
        self.m = bsz * t_len
        self.per_tile = t_len >= 256
        self.tm = 128

    def mod_arg(self, v):
        if self.per_tile:
            return v.reshape(self.bsz, 1, D_MODEL)
        return jnp.repeat(v, self.t_len, axis=0)

    def mod_spec(self):
        if self.per_tile:
            per = self.t_len // self.tm
            return pl.BlockSpec((None, 1, D_MODEL), lambda i: (i // per, 0, 0))
        return pl.BlockSpec((self.tm, D_MODEL), lambda i: (i, 0))

    def row_spec(self):
        return pl.BlockSpec((self.tm, D_MODEL), lambda i: (i, 0))

    @staticmethod
    def vec_spec():
        return pl.BlockSpec((1, D_MODEL), lambda i: (0, 0))


def _modulate(lay, x, shift, scale):
    return pl.pallas_call(
        _modulate_kernel,
        grid=(lay.m // lay.tm,),
        in_specs=[lay.row_spec(), lay.mod_spec(), lay.mod_spec()],
        out_specs=lay.row_spec(),
        out_shape=jax.ShapeDtypeStruct((lay.m, D_MODEL), BF16),
        compiler_params=_cparams(1),
        name="modulate",
    )(x, lay.mod_arg(shift), lay.mod_arg(scale))


def _post_norm(lay, x, out, gate, g, b, rho, nxt):
    g2, b2 = g.reshape(1, D_MODEL), b.reshape(1, D_MODEL)
    common = dict(grid=(lay.m // lay.tm,), compiler_params=_cparams(1))
    base_specs = [lay.row_spec(), lay.row_spec(), lay.mod_spec(), lay.vec_spec(), lay.vec_spec()]
    if nxt is None:
        return pl.pallas_call(
            functools.partial(_post_norm_kernel, rho=rho),
            in_specs=base_specs,
            out_specs=lay.row_spec(),
            out_shape=jax.ShapeDtypeStruct((lay.m, D_MODEL), F32),
            name="post_norm", **common,
        )(x, out, lay.mod_arg(gate), g2, b2), None
    shift, scale = nxt
    return pl.pallas_call(
        functools.partial(_post_norm_mod_kernel, rho=rho),
        in_specs=base_specs + [lay.mod_spec(), lay.mod_spec()],
        out_specs=[lay.row_spec(), lay.row_spec()],
        out_shape=[jax.ShapeDtypeStruct((lay.m, D_MODEL), F32), jax.ShapeDtypeStruct((lay.m, D_MODEL), BF16)],
        name="post_norm_mod", **common,
    )(x, out, lay.mod_arg(gate), g2, b2, lay.mod_arg(shift), lay.mod_arg(scale))


def _dot(a, b, precision=None):
    return jnp.dot(a, b, precision=precision, preferred_element_type=F32)


def _dot_nt(a, b, precision=None):
    return lax.dot_general(a, b, (((1,), (1,)), ((), ())), precision=precision, preferred_element_type=F32)


def _dot_tn(a, b):
    return lax.dot_general(a, b, (((0,), (0,)), ((), ())), preferred_element_type=F32)


def _sigmoid(x):
    return 1.0 / (1.0 + jnp.exp(-x))


def _silu(x):
    return x * _sigmoid(x)


def _softplus(x):
    return jnp.maximum(x, 0.0) + jnp.log1p(jnp.exp(-jnp.abs(x)))


def _neg_expm1(x):
    return -jnp.tanh(0.5 * x) * (jnp.exp(x) + 1.0)


def _iota2(shape, axis):
    return lax.broadcasted_iota(jnp.int32, shape, axis)


class _SeqLayout:
    def __init__(self, bsz, t_len, chunk_max):
        self.bsz, self.t_len = bsz, t_len
        blk = math.gcd(t_len, chunk_max)
        if blk % SUBLANES == 0:
            self.t_pad, self.chunk, self.valid = t_len, blk, blk
        else:
            assert blk == t_len and t_len < SUBLANES
            self.t_pad, self.chunk, self.valid = SUBLANES, SUBLANES, t_len
        self.tb = min(self.t_pad, 256)
        assert self.t_pad % self.tb == 0 and self.tb % self.chunk == 0
        self.nt = self.t_pad // self.tb
        self.n_chunks = self.tb // self.chunk
        self.m_pad = bsz * self.t_pad


def _row_mask(chunk, valid):
    return None if valid == chunk else (_iota2((chunk, 1), 0) < valid)


def _hgrn_kernel(q_ref, f_ref, i_ref, g_ref, s0_ref, lb_ref, nw_ref, o_ref, s_ref, st_ref, *, chunk, valid, n_chunks):
    hg = HEADS_PER_STEP
    t = pl.program_id(2)

    @pl.when(t == 0)
    def _():
        for h in range(hg):
            st_ref[h] = s0_ref[h].T

    lb = lb_ref[...]
    nw = nw_ref[...]
    row = _iota2((chunk, chunk), 0)
    col = _iota2((chunk, chunk), 1)
    causal = row >= col
    tri = causal.astype(F32)
    mask = _row_mask(chunk, valid)

    def body(c, carry):
        r = pl.ds(pl.multiple_of(c * chunk, chunk), chunk)
        f = lb + (1.0 - lb) * _sigmoid(f_ref[r, :])
        log_f = jnp.log(f)
        k = 1.0 - f
        v = i_ref[r, :]
        if mask is not None:
            log_f = jnp.where(mask, log_f, 0.0)
            v = jnp.where(mask, v, 0.0)
        cum = _dot(tri, log_f, HI)
        cum_last = cum[chunk - 1:chunk, :]
        q_dec = (_silu(q_ref[r, :]) * jnp.exp(cum)).astype(BF16)
        k_inv = (k * jnp.exp(-cum)).astype(BF16)
        k_end = (k * jnp.exp(cum_last - cum)).astype(BF16)
        g_end = jnp.exp(cum_last)
        vb = v.astype(BF16)
        gate = _silu(g_ref[r, :])
        for h in range(hg):
            sl = slice(h * DK_A, (h + 1) * DK_A)
            st = st_ref[h]
            scores = jnp.where(causal, _dot_nt(q_dec[:, sl], k_inv[:, sl]), 0.0)
            o = _dot(scores.astype(BF16), vb[:, sl]) + _dot_nt(q_dec[:, sl], st.astype(BF16))
            st_ref[h] = st * g_end[:, sl] + _dot_tn(vb[:, sl], k_end[:, sl])
            ms = jnp.mean(o * o, axis=-1, keepdims=True)
            o_ref[r, sl] = (o * lax.rsqrt(ms + RMS_EPS) * nw[:, sl] * gate[:, sl]).astype(o_ref.dtype)
        return carry

    lax.fori_loop(0, n_chunks, body, 0)

    @pl.when(t == pl.num_programs(2) - 1)
    def _():
        for h in range(hg):
            s_ref[h] = st_ref[h].T


def _hgrn(sq, u, s0, lb, norm_w):
    hg = HEADS_PER_STEP
    wl = hg * DK_A
    nsec = (H_A * DK_A) // wl

    def sec(s):
        return pl.BlockSpec((sq.tb, wl), lambda b, g, t: (b * sq.nt + t, s * nsec + g))

    state_spec = pl.BlockSpec((None, hg, DK_A, DV_A), lambda b, g, t: (b, g, 0, 0))
    vec_spec = pl.BlockSpec((1, wl), lambda b, g, t: (0, g))
    return pl.pallas_call(
        functools.partial(_hgrn_kernel, chunk=sq.chunk, valid=sq.valid, n_chunks=sq.n_chunks),
        grid=(sq.bsz, nsec, sq.nt),
        in_specs=[sec(0), sec(1), sec(2), sec(3), state_spec, vec_spec, vec_spec],
        out_specs=[pl.BlockSpec((sq.tb, wl), lambda b, g, t: (b * sq.nt + t, g)), state_spec],
        out_shape=[jax.ShapeDtypeStruct((sq.m_pad, H_A * DV_A), BF16),
                   jax.ShapeDtypeStruct((sq.bsz, H_A, DK_A, DV_A), F32)],
        scratch_shapes=[pltpu.VMEM((hg, DV_A, DK_A), F32)],
        compiler_params=_cparams(3),
        name="hgrn",
    )(u, u, u, u, s0, lb.reshape(1, -1), norm_w.reshape(1, -1))


def _delta_pre_kernel(qr_ref, kr_ref, vr_ref, cs_ref, w_ref, q_ref, k_ref, v_ref, carry_ref, *, tb):
    t = pl.program_id(1)
    width = H_B * DK_B

    @pl.when(t == 0)
    def _():
        for s in range(3):
            carry_ref[s] = cs_ref[:, s * width:(s + 1) * width]

    for s, (x_ref, y_ref) in enumerate(((qr_ref, q_ref), (kr_ref, k_ref), (vr_ref, v_ref))):
        w = w_ref[:, s * width:(s + 1) * width]
        x = x_ref[...]
        xx = jnp.concatenate([carry_ref[s], x], axis=0)
        y = w[CONV_W - 1:CONV_W] * x
        for j in range(1, CONV_W):
            y = y + w[CONV_W - 1 - j:CONV_W - j] * xx[SUBLANES - j:SUBLANES - j + tb]
        carry_ref[s] = x[tb - SUBLANES:tb]
        y = _silu(y)
        if s == 2:
            y_ref[...] = y
        else:
            scale = DK_B ** -0.5 if s == 0 else 1.0
            for h in range(H_B):
                sl = slice(h * DK_B, (h + 1) * DK_B)
                yh = y[:, sl]
                ss = jnp.sum(yh * yh, axis=-1, keepdims=True)
                y_ref[:, sl] = yh * (lax.rsqrt(ss + RMS_EPS) * scale)


def _delta_pre(sq, u, conv_state, conv_w):
    width = H_B * DK_B
    first = (4 * H_A * DK_A) // width

    def sec(s):
        return pl.BlockSpec((sq.tb, width), lambda b, t: (b * sq.nt + t, first + s))

    out_spec = pl.BlockSpec((sq.tb, width), lambda b, t: (b * sq.nt + t, 0))
    cs = jnp.pad(conv_state.astype(F32), ((0, 0), (SUBLANES - (CONV_W - 1), 0), (0, 0)))
    return pl.pallas_call(
        functools.partial(_delta_pre_kernel, tb=sq.tb),
        grid=(sq.bsz, sq.nt),
        in_specs=[sec(0), sec(1), sec(2),
                  pl.BlockSpec((None, SUBLANES, C_B), lambda b, t: (b, 0, 0)),
                  pl.BlockSpec((CONV_W, C_B), lambda b, t: (0, 0))],
        out_specs=[out_spec, out_spec, out_spec],
        out_shape=[jax.ShapeDtypeStruct((sq.m_pad, width), F32)] * 3,
        scratch_shapes=[pltpu.VMEM((3, SUBLANES, width), F32)],
        compiler_params=_cparams(2),
        name="delta_pre",
    )(u, u, u, cs, conv_w.astype(F32))


def _delta_kernel(q_ref, k_ref, v_ref, z_ref, gt_ref, s0_ref, alog_ref, dtb_ref, nw_ref, o_ref, s_ref, *,
                  chunk, valid, n_chunks):
    hg = HEADS_PER_STEP
    t = pl.program_id(2)

    @pl.when(t == 0)
    def _():
        s_ref[...] = s0_ref[...]

    neg_a = -jnp.exp(alog_ref[...])
    dtb = dtb_ref[...]
    nw = nw_ref[...]
    row = _iota2((chunk, chunk), 0)
    col = _iota2((chunk, chunk), 1)
    incl = row >= col
    strict = row > col
    tri = incl.astype(F32)
    eye = (row == col).astype(F32)
    pick = (_iota2((SUBLANES, V7X_LANES), 0) == _iota2((SUBLANES, V7X_LANES), 1)).astype(F32)
    mask = _row_mask(chunk, valid)
    n_sq = max(int(math.log2(chunk)) - 1, 0)

    def body(c, carry):
        r = pl.ds(pl.multiple_of(c * chunk, chunk), chunk)
        gt = gt_ref[r, :]
        log_g = neg_a * _softplus(gt + dtb)
        beta_all = _sigmoid(gt)
        if mask is not None:
            log_g = jnp.where(mask, log_g, 0.0)
            beta_all = jnp.where(mask, beta_all, 0.0)
        cum = _dot(tri, log_g, HI)
        cum_t = _dot_nt(pick, cum, HI)
        for h in range(hg):
            sl = slice(h * DK_B, (h + 1) * DK_B)
            c_col = cum[:, h:h + 1]
            c_row = cum_t[h:h + 1, :]
            c_last = cum[chunk - 1:chunk, h:h + 1]
            beta = beta_all[:, GATE_BETA_LANE + h:GATE_BETA_LANE + h + 1]
            decay = jnp.where(incl, jnp.exp(jnp.where(incl, c_col - c_row, 0.0)), 0.0)
            qh, kh, vh = q_ref[r, sl], k_ref[r, sl], v_ref[r, sl]
            qb, kb = qh.astype(BF16), kh.astype(BF16)
            nmat = jnp.where(strict, beta * _dot_nt(kb, kb) * decay, 0.0)
            inv = eye - nmat
            power = nmat
            for _ in range(n_sq):
                power = _dot(power, power, HI)
                inv = inv + _dot(inv, power, HI)
            e_col = jnp.exp(c_col)
            rhs = jnp.concatenate([beta * vh, (beta * e_col) * kh], axis=1)
            w = _dot(inv, rhs, HI)
            w_v, w_k = w[:, :DV_B], w[:, DV_B:]
            state = s_ref[h]
            sb = state.astype(BF16)
            u = w_v - _dot(w_k.astype(BF16), sb)
            ub = u.astype(BF16)
            qk = _dot_nt(qb, kb) * decay
            o = _dot((qh * e_col).astype(BF16), sb) + _dot(qk.astype(BF16), ub)
            k_end = (kh * jnp.exp(c_last - c_col)).astype(BF16)
            s_ref[h] = jnp.exp(c_last) * state + _dot_tn(k_end, ub)
            ms = jnp.mean(o * o, axis=-1, keepdims=True)
            o_ref[r, sl] = (o * lax.rsqrt(ms + RMS_EPS) * nw * _silu(z_ref[r, sl])).astype(o_ref.dtype)
        return carry

    lax.fori_loop(0, n_chunks, body, 0)


def _delta_gate_lanes(v, fill):
    hg = HEADS_PER_STEP
    v = v.astype(F32).reshape(H_B // hg, 1, hg)
    return jnp.pad(v, ((0, 0), (0, 0), (0, V7X_LANES - hg)), constant_values=fill)


def _delta_gate_weights(w_in0):
    hg = HEADS_PER_STEP
    ab = w_in0[:, IN0_MAIN:IN0_MAIN + H_B]
    bb = w_in0[:, IN0_MAIN + H_B:]
    tiles = []
    for g in range(H_B // hg):
        tile = jnp.zeros((D_MODEL, V7X_LANES), w_in0.dtype)
        tile = tile.at[:, :hg].set(ab[:, g * hg:(g + 1) * hg])
        tile = tile.at[:, GATE_BETA_LANE:GATE_BETA_LANE + hg].set(bb[:, g * hg:(g + 1) * hg])
        tiles.append(tile)
    return jnp.concatenate(tiles, axis=1).astype(BF16)


def _delta(sq, q, k, v, u, gates, s0, a_log, dt_bias, norm_w):
    hg = HEADS_PER_STEP
    wl = hg * DK_B
    ngrp = H_B // hg
    z_first = (4 * H_A * DK_A + C_B) // wl

    def rows(first):
        return pl.BlockSpec((sq.tb, wl), lambda b, g, t: (b * sq.nt + t, first + g))

    state_spec = pl.BlockSpec((None, hg, DK_B, DV_B), lambda b, g, t: (b, g, 0, 0))
    lane_spec = pl.BlockSpec((None, 1, V7X_LANES), lambda b, g, t: (g, 0, 0))
    return pl.pallas_call(
        functools.partial(_delta_kernel, chunk=sq.chunk, valid=sq.valid, n_chunks=sq.n_chunks),
        grid=(sq.bsz, ngrp, sq.nt),
        in_specs=[rows(0), rows(0), rows(0), rows(z_first),
                  pl.BlockSpec((sq.tb, V7X_LANES), lambda b, g, t: (b * sq.nt + t, g)),
                  state_spec, lane_spec, lane_spec,
                  pl.BlockSpec((1, DV_B), lambda b, g, t: (0, 0))],
        out_specs=[rows(0), state_spec],
        out_shape=[jax.ShapeDtypeStruct((sq.m_pad, H_B * DV_B), BF16),
                   jax.ShapeDtypeStruct((sq.bsz, H_B, DK_B, DV_B), F32)],
        compiler_params=_cparams(3),
        name="delta",
    )(q, k, v, u, gates, s0, _delta_gate_lanes(a_log, 0.0), _delta_gate_lanes(dt_bias, 0.0),
      norm_w.astype(F32).reshape(1, DV_B))


def _rglru_kernel(x_ref, gate_ref, cs_ref, h0_ref, cw_ref, cb_ref, wa_ref, wx_ref, ba_ref, bx_ref, lam_ref,
                  y_ref, hl_ref, carry_ref, h_ref, a_ref, b_ref, *, tb, valid_rows, first_pos_is_zero):
    t = pl.program_id(1)

    @pl.when(t == 0)
    def _():
        carry_ref[...] = cs_ref[...]
        h_ref[...] = h0_ref[...]

    w = cw_ref[...]
    x = x_ref[...]
    xx = jnp.concatenate([carry_ref[...], x], axis=0)
    xr = w[CONV_W - 1:CONV_W] * x + cb_ref[...]
    for j in range(1, CONV_W):
        xr = xr + w[CONV_W - 1 - j:CONV_W - j] * xx[SUBLANES - j:SUBLANES - j + tb]
    carry_ref[...] = x[tb - SUBLANES:tb]
    neg_c_sp = -RG_C * _softplus(-lam_ref[...])
    for n in range(RG_BLOCKS):
        sl = slice(n * RG_BW, (n + 1) * RG_BW)
        xn = xr[:, sl]
        xb = xn.astype(BF16)
        r = _sigmoid(_dot(xb, wa_ref[n]) + ba_ref[:, sl])
        i = _sigmoid(_dot(xb, wx_ref[n]) + bx_ref[:, sl])
        log_a = neg_c_sp[:, sl] * r
        mult = jnp.sqrt(_neg_expm1(2.0 * log_a))
        if first_pos_is_zero:
            is_first = jnp.logical_and(t == 0, _iota2((tb, 1), 0) == 0)
            mult = jnp.where(is_first, 1.0, mult)
        a_ref[:, sl] = jnp.exp(log_a)
        b_ref[:, sl] = mult * i * xn

    def step(s, h):
        h = a_ref[pl.ds(s, 1), :] * h + b_ref[pl.ds(s, 1), :]
        b_ref[pl.ds(s, 1), :] = h
        return h

    h_ref[...] = lax.fori_loop(0, valid_rows, step, h_ref[...])
    y_ref[...] = (b_ref[...] * jax.nn.gelu(gate_ref[...], approximate=True)).astype(y_ref.dtype)

    @pl.when(t == pl.num_programs(1) - 1)
    def _():
        hl_ref[...] = h_ref[...]


def _rglru(sq, u, conv_state, h0, p, first_pos_is_zero):
    def sec(s):
        return pl.BlockSpec((sq.tb, RG_WIDTH), lambda b, t: (b * sq.nt + t, s))

    vec = pl.BlockSpec((1, RG_WIDTH), lambda b, t: (0, 0))
    blk = pl.BlockSpec((RG_BLOCKS, RG_BW, RG_BW), lambda b, t: (0, 0, 0))
    state = pl.BlockSpec((None, 1, RG_WIDTH), lambda b, t: (b, 0, 0))
    cs = jnp.pad(conv_state.astype(F32), ((0, 0), (SUBLANES - (CONV_W - 1), 0), (0, 0)))
    valid_rows = sq.tb if sq.valid == sq.chunk else sq.valid
    y, h_last = pl.pallas_call(
        functools.partial(_rglru_kernel, tb=sq.tb, valid_rows=valid_rows, first_pos_is_zero=first_pos_is_zero),
        grid=(sq.bsz, sq.nt),
        in_specs=[sec(0), sec(1),
                  pl.BlockSpec((None, SUBLANES, RG_WIDTH), lambda b, t: (b, 0, 0)), state,
                  pl.BlockSpec((CONV_W, RG_WIDTH), lambda b, t: (0, 0)), vec, blk, blk, vec, vec, vec],
        out_specs=[pl.BlockSpec((sq.tb, RG_WIDTH), lambda b, t: (b * sq.nt + t, 0)), state],
        out_shape=[jax.ShapeDtypeStruct((sq.m_pad, RG_WIDTH), BF16),
                   jax.ShapeDtypeStruct((sq.bsz, 1, RG_WIDTH), F32)],
        scratch_shapes=[pltpu.VMEM((SUBLANES, RG_WIDTH), F32), pltpu.VMEM((1, RG_WIDTH), F32),
                        pltpu.VMEM((sq.tb, RG_WIDTH), F32), pltpu.VMEM((sq.tb, RG_WIDTH), F32)],
        compiler_params=_cparams(2),
        name="rglru",
    )(u, u, cs, h0.astype(F32).reshape(sq.bsz, 1, RG_WIDTH), p['rg_conv_w'].astype(F32),
      p['rg_conv_b'].astype(F32).reshape(1, -1), p['rg_wa'].astype(BF16), p['rg_wx'].astype(BF16),
      p['rg_ba'].astype(F32).reshape(1, -1), p['rg_bx'].astype(F32).reshape(1, -1),
      p['rg_lambda'].astype(F32).reshape(1, -1))
    return y, h_last.reshape(sq.bsz, RG_WIDTH)


def _ret_kernel(q_ref, k_ref, v_ref, g_ref, cos_ref, sin_ref, s0_ref, o_ref, s_ref, *, chunk, valid, n_chunks):
    t = pl.program_id(1)

    @pl.when(t == 0)
    def _():
        s_ref[...] = s0_ref[...]

    row = _iota2((chunk, chunk), 0)
    col = _iota2((chunk, chunk), 1)
    causal = row >= col
    seen = jnp.minimum(_iota2((chunk, 1), 0) + 1, valid).astype(F32)
    dist = (row - col).astype(F32)
    mask = _row_mask(chunk, valid)
    log_gamma = [math.log1p(-2.0 ** (-5.0 - h)) for h in range(H_D)]

    def body(c, carry):
        r = pl.ds(pl.multiple_of(c * chunk, chunk), chunk)
        cos = cos_ref[r, :]
        sin = sin_ref[r, :]
        for h in range(H_D):
            lg = log_gamma[h]
            slk = slice(h * DK_D, (h + 1) * DK_D)
            slv = slice(h * DV_D, (h + 1) * DV_D)
            qh, kh = q_ref[r, slk], k_ref[r, slk]
            qr = qh * cos + pltpu.roll(qh, DK_D // 2, axis=1) * sin
            kr = (kh * cos + pltpu.roll(kh, DK_D // 2, axis=1) * sin) * (DK_D ** -0.5)
            vh = v_ref[r, slv]
            if mask is not None:
                vh = jnp.where(mask, vh, 0.0)
            vb = vh.astype(BF16)
            scores = jnp.where(causal, _dot_nt(qr.astype(BF16), kr.astype(BF16)) * jnp.exp(dist * lg), 0.0)
            state = s_ref[h]
            o = _dot(scores.astype(BF16), vb) + _dot((qr * jnp.exp(seen * lg)).astype(BF16), state.astype(BF16))
            k_end = (kr * jnp.exp((valid - seen) * lg)).astype(BF16)
            s_ref[h] = math.exp(valid * lg) * state + _dot_tn(k_end, vb)
            mu = jnp.mean(o, axis=-1, keepdims=True)
            oc = o - mu
            var = jnp.mean(oc * oc, axis=-1, keepdims=True)
            o_ref[r, slv] = (oc * lax.rsqrt(var + LN_EPS) * _silu(g_ref[r, slv])).astype(o_ref.dtype)
        return carry

    lax.fori_loop(0, n_chunks, body, 0)


def _rotary_tables(sq, start):
    half = DK_D // 2
    inv = ROPE_BASE ** (-jnp.arange(half, dtype=F32) / half)
    pos = start + jnp.arange(sq.t_pad, dtype=jnp.int32)
    ang = pos.astype(F32)[:, None] * inv
    cos, sin = jnp.cos(ang), jnp.sin(ang)
    cos2 = jnp.concatenate([cos, cos], axis=-1)
    sin2 = jnp.concatenate([-sin, sin], axis=-1)
    return cos2, sin2


def _retention(sq, u, s0, start):
    wq = H_D * DK_D
    wv = H_D * DV_D
    q_first = (2 * RG_WIDTH) // wq
    v_first = (2 * RG_WIDTH + 2 * wq) // wv

    def rows(width, first):
        return pl.BlockSpec((sq.tb, width), lambda b, t: (b * sq.nt + t, first))

    tab = pl.BlockSpec((sq.tb, DK_D), lambda b, t: (t, 0))
    state = pl.BlockSpec((None, H_D, DK_D, DV_D), lambda b, t: (b, 0, 0, 0))
    cos2, sin2 = _rotary_tables(sq, start)
    return pl.pallas_call(
        functools.partial(_ret_kernel, chunk=sq.chunk, valid=sq.valid, n_chunks=sq.n_chunks),
        grid=(sq.bsz, sq.nt),
        in_specs=[rows(wq, q_first), rows(wq, q_first + 1), rows(wv, v_first), rows(wv, v_first + 1), tab, tab, state],
        out_specs=[rows(wv, 0), state],
        out_shape=[jax.ShapeDtypeStruct((sq.m_pad, wv), BF16),
                   jax.ShapeDtypeStruct((sq.bsz, H_D, DK_D, DV_D), F32)],
        compiler_params=_cparams(2),
        name="retention",
    )(u, u, u, u, cos2, sin2, s0.astype(F32))


def _pad_rows(sq, a):
    if sq.t_pad == sq.t_len:
        return a
    a = a.reshape(sq.bsz, sq.t_len, a.shape[-1])
    return jnp.pad(a, ((0, 0), (0, sq.t_pad - sq.t_len), (0, 0))).reshape(sq.m_pad, a.shape[-1])


def _real_rows(sq, a):
    if sq.t_pad == sq.t_len:
        return a
    return a.reshape(sq.bsz, sq.t_pad, a.shape[-1])[:, :sq.t_len].reshape(sq.bsz * sq.t_len, a.shape[-1])


def _last_inputs(sq, u, first_col, width):
    assert sq.t_len >= CONV_W - 1
    u3 = u.reshape(sq.bsz, sq.t_pad, u.shape[-1])
    return u3[:, sq.t_len - (CONV_W - 1):sq.t_len, first_col:first_col + width]


def _mixer_ab(h, bsz, t_len, layer, s_hgrn, s_delta, s_dconv, p, wb):
    sq_a = _SeqLayout(bsz, t_len, CHUNK_A)
    sq_b = _SeqLayout(bsz, t_len, CHUNK_B)
    assert sq_a.t_pad == sq_b.t_pad
    hp = _pad_rows(sq_a, h)
    tm = min(sq_a.m_pad, 1024)
    u = _matmul(hp, wb['w_in0'], IN0_MAIN, F32, tm, 1024)
    gates = _matmul(hp, wb['w_in0_gates'], wb['w_in0_gates'].shape[1], F32, tm, wb['w_in0_gates'].shape[1])
    lb_all = jnp.cumsum(jax.nn.softmax(p['hgrn_lb_logits'].astype(F32), axis=0), axis=0)
    o_a, hgrn_new = _hgrn(sq_a, u, s_hgrn.astype(F32), lb_all[layer], p['hgrn_norm_w'].astype(F32))
    q, k, v = _delta_pre(sq_b, u, s_dconv, p['delta_conv_w'])
    o_b, delta_new = _delta(sq_b, q, k, v, u, gates, s_delta.astype(F32), p['delta_a_log'], p['delta_dt_bias'],
                            p['delta_norm_w'])
    dconv_new = _last_inputs(sq_b, u, 4 * H_A * DK_A, C_B)
    return _real_rows(sq_a, o_a), _real_rows(sq_b, o_b), (hgrn_new, delta_new, dconv_new)


def _mixer_cd(h, bsz, t_len, start, s_rg, s_rgconv, s_ret, p, wb):
    sq = _SeqLayout(bsz, t_len, CHUNK_D)
    hp = _pad_rows(sq, h)
    u = _matmul(hp, wb['w_in1'], IN1, F32, min(sq.m_pad, 1024), 1024)
    y_rg, rg_new = _rglru(sq, u, s_rgconv, s_rg, p, first_pos_is_zero=(start == 0))
    o_d, ret_new = _retention(sq, u, s_ret, start)
    rgconv_new = _last_inputs(sq, u, 0, RG_WIDTH)
    return _real_rows(sq, y_rg), _real_rows(sq, o_d), (rg_new, rgconv_new, ret_new)


def _trunk(x, mod_all, start, states, p, wb):
    s_hgrn, s_delta, s_dconv, s_rg, s_rgconv, s_ret = states
    bsz, t_len, _ = x.shape
    lay = _RowLayout(bsz, t_len)
    m = lay.m
    tm_mm = min(m, 1024)
    tm_ffn = min(m, 2048)
    tm_out = 512
    x = x.reshape(m, D_MODEL)

    def mod_of(layer, j, which):
        return mod_all[layer].reshape(bsz, 3, 3, D_MODEL)[:, j, which]

    h = _modulate(lay, x, mod_of(0, 0, 0), mod_of(0, 0, 1))
    for layer in range(DEPTH):
        lg, lb = p['ln_g'][layer], p['ln_b'][layer]
        mid = _ffn_in(h, wb['ffn_w_in'], layer, 0, tm_ffn)
        ffn = _matmul(mid, wb['ffn_w_out'][layer][0], D_MODEL, F32, tm_out, 512)
        x, h = _post_norm(lay, x, ffn, mod_of(layer, 0, 2), lg[0], lb[0], 0.5,
                          (mod_of(layer, 1, 0), mod_of(layer, 1, 1)))
        if layer % 2 == 0:
            o_1, o_2, (s_hgrn, s_delta, s_dconv) = _mixer_ab(h, bsz, t_len, layer, s_hgrn, s_delta, s_dconv, p, wb)
            w_o = wb['w_out0']
        else:
            o_1, o_2, (s_rg, s_rgconv, s_ret) = _mixer_cd(h, bsz, t_len, start, s_rg, s_rgconv, s_ret, p, wb)
            w_o = wb['w_out1']
        mix = _matmul_halves(o_1, o_2, w_o, tm_mm, 1024)
        x, h = _post_norm(lay, x, mix, mod_of(layer, 1, 2), lg[1], lb[1], 1.0,
                          (mod_of(layer, 2, 0), mod_of(layer, 2, 1)))
        mid = _ffn_in(h, wb['ffn_w_in'], layer, 1, tm_ffn)
        ffn = _matmul(mid, wb['ffn_w_out'][layer][1], D_MODEL, F32, tm_out, 512)
        nxt = (mod_of(layer + 1, 0, 0), mod_of(layer + 1, 0, 1)) if layer + 1 < DEPTH else None
        x, h = _post_norm(lay, x, ffn, mod_of(layer, 2, 2), lg[2], lb[2], 0.5, nxt)
    return x.reshape(bsz, t_len, D_MODEL), (s_hgrn, s_delta, s_dconv, s_rg, s_rgconv, s_ret)


def kernel(x_prompt, x_sample, c_prompt, c_sample, state_hgrn, state_delta, state_delta_conv, state_rglru, state_rglru_conv, state_ret, ada_w, ada_b, ln_g, ln_b, ffn_w_in, ffn_w_out, w_in0, w_out0, hgrn_lb_logits, hgrn_norm_w, delta_conv_w, delta_a_log, delta_dt_bias, delta_norm_w, w_in1, w_out1, rg_conv_w, rg_conv_b, rg_wa, rg_ba, rg_wx, rg_bx, rg_lambda):
    p = {
        'ln_g': ln_g, 'ln_b': ln_b, 'hgrn_lb_logits': hgrn_lb_logits, 'hgrn_norm_w': hgrn_norm_w,
        'delta_conv_w': delta_conv_w, 'delta_a_log': delta_a_log, 'delta_dt_bias': delta_dt_bias,
        'delta_norm_w': delta_norm_w, 'rg_conv_w': rg_conv_w, 'rg_conv_b': rg_conv_b, 'rg_wa': rg_wa,
        'rg_ba': rg_ba, 'rg_wx': rg_wx, 'rg_bx': rg_bx, 'rg_lambda': rg_lambda,
    }
    wb = {
        'ffn_w_in': ffn_w_in,
        'ffn_w_out': [[ffn_w_out[l, s].astype(BF16) for s in range(2)] for l in range(DEPTH)],
        'w_in0': w_in0.astype(BF16),
        'w_in0_gates': _delta_gate_weights(w_in0),
        'w_out0': w_out0.astype(BF16),
        'w_in1': w_in1.astype(BF16),
        'w_out1': w_out1.astype(BF16),
    }
    nb, ns = x_prompt.shape[0], x_sample.shape[0]
    c_all = jnp.concatenate([c_prompt, c_sample], axis=0).astype(F32)
    rows = nb + ns
    rows_pad = -(-rows // 16) * 16
    c_act = jnp.pad(jax.nn.silu(c_all), ((0, rows_pad - rows), (0, 0))).astype(BF16)
    mod = _ada(c_act, ada_w, ada_b)
    mod_prompt, mod_sample = mod[:, :nb], mod[:, nb:rows]

    prompt_states = (
        jnp.zeros((nb, H_A, DK_A, DV_A), F32),
        jnp.zeros((nb, H_B, DK_B, DV_B), F32),
        jnp.zeros((nb, CONV_W - 1, C_B), F32),
        jnp.zeros((nb, RG_WIDTH), F32),
        jnp.zeros((nb, CONV_W - 1, RG_WIDTH), F32),
        jnp.zeros((nb, H_D, DK_D, DV_D), F32),
    )
    y_prompt, ps = _trunk(x_prompt, mod_prompt, 0, prompt_states, p, wb)
    sample_states = (state_hgrn, state_delta, state_delta_conv, state_rglru, state_rglru_conv, state_ret)
    y_sample, ss = _trunk(x_sample, mod_sample, PAST_LEN, sample_states, p, wb)
    return (y_prompt, y_sample) + tuple(ps) + tuple(ss)
```

```python
import functools
import math

import jax
import jax.numpy as jnp
import numpy as np
from jax import lax
from jax.experimental import pallas as pl
from jax.experimental.pallas import tpu as pltpu

D_MODEL = 4096
DEPTH = 2
PAST_LEN = 16384
MIX_HALF = D_MODEL // 2
DK_A = 128
DV_A = 128
H_A = MIX_HALF // DV_A
DK_B = 128
DV_B = 128
H_B = MIX_HALF // DV_B
CONV_W = 4
C_B = 2 * H_B * DK_B + H_B * DV_B
RG_WIDTH = MIX_HALF
RG_BW = 128
RG_BLOCKS = RG_WIDTH // RG_BW
RG_C = 8.0
DK_D = 128
DV_D = 256
H_D = MIX_HALF // DV_D
D_FF = 11008
CHUNK_A = 16
CHUNK_B = 64
CHUNK_D = 64
ROPE_BASE = 10000.0
LN_EPS = 1e-5
RMS_EPS = 1e-6
ALPHA = (2 * DEPTH) ** 0.25

IN0_SIZES = (H_A * DK_A, H_A * DK_A, H_A * DV_A, H_A * DV_A, C_B, H_B * DV_B, H_B, H_B)
IN0 = sum(IN0_SIZES)
IN0_MAIN = IN0 - 2 * H_B
IN1_SIZES = (RG_WIDTH, RG_WIDTH, H_D * DK_D, H_D * DK_D, H_D * DV_D, H_D * DV_D)
IN1 = sum(IN1_SIZES)

F32 = jnp.float32
BF16 = jnp.bfloat16

V7X_VMEM_LIMIT_BYTES = 56 * 1024 * 1024
V7X_LANES = 128
SUBLANES = 8
FFN_TN = 256
HEADS_PER_STEP = 4
GATE_BETA_LANE = 16


def _cparams(n_axes):
    return pltpu.CompilerParams(dimension_semantics=("arbitrary",) * n_axes,
                                vmem_limit_bytes=V7X_VMEM_LIMIT_BYTES)


def _mm_kernel(x_ref, w_ref, o_ref):
    o_ref[...] = jnp.dot(x_ref[...], w_ref[...], preferred_element_type=F32).astype(o_ref.dtype)


def _matmul(x, w, n_cols, out_dtype, tm, tn, lead=()):
    m, k = x.shape
    assert m % tm == 0 and n_cols % tn == 0 and w.shape[len(lead)] == k
    return pl.pallas_call(
        _mm_kernel,
        grid=(m // tm, n_cols // tn),
        in_specs=[pl.BlockSpec((tm, k), lambda i, j: (i, 0)),
                  pl.BlockSpec((None,) * len(lead) + (k, tn), lambda i, j: lead + (0, j))],
        out_specs=pl.BlockSpec((tm, tn), lambda i, j: (i, j)),
        out_shape=jax.ShapeDtypeStruct((m, n_cols), out_dtype),
        compiler_params=_cparams(2),
        name="matmul",
    )(x, w)


def _mm2_kernel(xa_ref, xb_ref, wa_ref, wb_ref, o_ref):
    o_ref[...] = (jnp.dot(xa_ref[...], wa_ref[...], preferred_element_type=F32)
                  + jnp.dot(xb_ref[...], wb_ref[...], preferred_element_type=F32)).astype(o_ref.dtype)


def _matmul_halves(xa, xb, w, tm, tn):
    m, kh = xa.shape
    n = w.shape[1]
    return pl.pallas_call(
        _mm2_kernel,
        grid=(m // tm, n // tn),
        in_specs=[pl.BlockSpec((tm, kh), lambda i, j: (i, 0)), pl.BlockSpec((tm, kh), lambda i, j: (i, 0)),
                  pl.BlockSpec((kh, tn), lambda i, j: (0, j)), pl.BlockSpec((kh, tn), lambda i, j: (1, j))],
        out_specs=pl.BlockSpec((tm, tn), lambda i, j: (i, j)),
        out_shape=jax.ShapeDtypeStruct((m, n), F32),
        compiler_params=_cparams(2),
        name="matmul_halves",
    )(xa, xb, w, w)


def _swiglu_kernel(x_ref, wg_ref, wu_ref, o_ref):
    x = x_ref[...]
    gate = jnp.dot(x, wg_ref[...].astype(BF16), preferred_element_type=F32)
    up = jnp.dot(x, wu_ref[...].astype(BF16), preferred_element_type=F32)
    o_ref[...] = (gate * jax.nn.sigmoid(gate) * up).astype(o_ref.dtype)


def _ffn_in(x, ffn_w_in, layer, half, tm):
    m, k = x.shape
    n_tiles = D_FF // FFN_TN
    return pl.pallas_call(
        _swiglu_kernel,
        grid=(m // tm, n_tiles),
        in_specs=[pl.BlockSpec((tm, k), lambda i, j: (i, 0), pipeline_mode=pl.Buffered(1)),
                  pl.BlockSpec((None, None, k, FFN_TN), lambda i, j: (layer, half, 0, j)),
                  pl.BlockSpec((None, None, k, FFN_TN), lambda i, j: (layer, half, 0, j + n_tiles))],
        out_specs=pl.BlockSpec((tm, FFN_TN), lambda i, j: (i, j)),
        out_shape=jax.ShapeDtypeStruct((m, D_FF), BF16),
        compiler_params=_cparams(2),
        name="ffn_in",
    )(x, ffn_w_in, ffn_w_in)


def _ada_kernel(c_ref, w_ref, b_ref, o_ref):
    o_ref[...] = jnp.dot(c_ref[...], w_ref[...].astype(BF16), preferred_element_type=F32) + b_ref[...]


def _ada(c_act, ada_w, ada_b, tn=512):
    r = c_act.shape[0]
    n = ada_w.shape[2]
    return pl.pallas_call(
        _ada_kernel,
        grid=(DEPTH, n // tn),
        in_specs=[pl.BlockSpec((r, D_MODEL), lambda l, j: (0, 0)),
                  pl.BlockSpec((None, D_MODEL, tn), lambda l, j: (l, 0, j)),
                  pl.BlockSpec((None, 1, tn), lambda l, j: (l, 0, j))],
        out_specs=pl.BlockSpec((None, r, tn), lambda l, j: (l, 0, j)),
        out_shape=jax.ShapeDtypeStruct((DEPTH, r, n), F32),
        compiler_params=_cparams(2),
        name="ada",
    )(c_act, ada_w, ada_b.reshape(DEPTH, 1, n))


def _modulate_kernel(x_ref, shift_ref, scale_ref, h_ref):
    h_ref[...] = (x_ref[...] * (1.0 + scale_ref[...]) + shift_ref[...]).astype(h_ref.dtype)


def _post_norm_body(x_ref, out_ref, gate_ref, g_ref, b_ref, rho):
    z = ALPHA * x_ref[...] + rho * gate_ref[...] * out_ref[...]
    mu = jnp.mean(z, axis=-1, keepdims=True)
    zc = z - mu
    var = jnp.mean(zc * zc, axis=-1, keepdims=True)
    return zc * lax.rsqrt(var + LN_EPS) * g_ref[...] + b_ref[...]


def _post_norm_mod_kernel(x_ref, out_ref, gate_ref, g_ref, b_ref, shift_ref, scale_ref, xn_ref, h_ref, *, rho):
    xn = _post_norm_body(x_ref, out_ref, gate_ref, g_ref, b_ref, rho)
    xn_ref[...] = xn
    h_ref[...] = (xn * (1.0 + scale_ref[...]) + shift_ref[...]).astype(h_ref.dtype)


def _post_norm_kernel(x_ref, out_ref, gate_ref, g_ref, b_ref, xn_ref, *, rho):
    xn_ref[...] = _post_norm_body(x_ref, out_ref, gate_ref, g_ref, b_ref, rho)


class _RowLayout:
    def __init__(self, bsz, t_len):
        self.bsz, self.t_len = bsz, t_len
        self.m = bsz * t_len
        self.per_tile = t_len >= 256
        self.tm = min(self.m, 128)

    def mod_arg(self, mod_layer):
        if self.per_tile:
            return mod_layer.reshape(self.bsz, 1, 9 * D_MODEL)
        return jnp.repeat(mod_layer, self.t_len, axis=0)

    def mod_spec(self, col):
        if self.per_tile:
            per = self.t_len // self.tm
            return pl.BlockSpec((None, 1, D_MODEL), lambda i: (i // per, 0, col))
        return pl.BlockSpec((self.tm, D_MODEL), lambda i: (i, col))

    def row_spec(self):
        return pl.BlockSpec((self.tm, D_MODEL), lambda i: (i, 0))

    @staticmethod
    def vec_spec():
        return pl.BlockSpec((1, D_MODEL), lambda i: (0, 0))


def _modulate(lay, x, mods, sub):
    return pl.pallas_call(
        _modulate_kernel,
        grid=(lay.m // lay.tm,),
        in_specs=[lay.row_spec(), lay.mod_spec(3 * sub), lay.mod_spec(3 * sub + 1)],
        out_specs=lay.row_spec(),
        out_shape=jax.ShapeDtypeStruct((lay.m, D_MODEL), BF16),
        compiler_params=_cparams(1),
        name="modulate",
    )(x, mods, mods)


def _post_norm(lay, x, out, mods, sub, g, b, rho, nxt):
    g2, b2 = g.reshape(1, D_MODEL), b.reshape(1, D_MODEL)
    common = dict(grid=(lay.m // lay.tm,), compiler_params=_cparams(1))
    base_specs = [lay.row_spec(), lay.row_spec(), lay.mod_spec(3 * sub + 2), lay.vec_spec(), lay.vec_spec()]
    if nxt is None:
        return pl.pallas_call(
            functools.partial(_post_norm_kernel, rho=rho),
            in_specs=base_specs,
            out_specs=lay.row_spec(),
            out_shape=jax.ShapeDtypeStruct((lay.m, D_MODEL), F32),
            name="post_norm", **common,
        )(x, out, mods, g2, b2), None
    nxt_mods, nxt_sub = nxt
    return pl.pallas_call(
        functools.partial(_post_norm_mod_kernel, rho=rho),
        in_specs=base_specs + [lay.mod_spec(3 * nxt_sub), lay.mod_spec(3 * nxt_sub + 1)],
        out_specs=[lay.row_spec(), lay.row_spec()],
        out_shape=[jax.ShapeDtypeStruct((lay.m, D_MODEL), F32), jax.ShapeDtypeStruct((lay.m, D_MODEL), BF16)],
        name="post_norm_mod", **common,
    )(x, out, mods, g2, b2, nxt_mods, nxt_mods)


def _dot(a, b):
    return jnp.dot(a, b, preferred_element_type=F32)


def _dot_nt(a, b):
    return lax.dot_general(a, b, (((1,), (1,)), ((), ())), preferred_element_type=F32)


def _dot_tn(a, b):
    return lax.dot_general(a, b, (((0,), (0,)), ((), ())), preferred_element_type=F32)


def _split_bf16(x, terms):
    parts = []
    for _ in range(terms - 1):
        hi = x.astype(BF16)
        parts.append(hi)
        x = x - hi.astype(F32)
    parts.append(x.astype(BF16))
    return parts


def _dot_split(a, b):
    ah, al = _split_bf16(a, 2)
    bh, bl = _split_bf16(b, 2)
    return _dot(ah, bh) + (_dot(ah, bl) + _dot(al, bh))


def _dot_select(sel, x, nt=False):
    f = _dot_nt if nt else _dot
    x1, x2, x3 = _split_bf16(x, 3)
    return f(sel, x1) + (f(sel, x2) + f(sel, x3))


def _sigmoid(x):
    return 1.0 / (1.0 + jnp.exp(-x))


def _silu(x):
    return x * _sigmoid(x)


def _softplus(x):
    return jnp.maximum(x, 0.0) + jnp.log1p(jnp.exp(-jnp.abs(x)))


def _neg_expm1(x):
    return -jnp.tanh(0.5 * x) * (jnp.exp(x) + 1.0)


def _iota2(shape, axis):
    return lax.broadcasted_iota(jnp.int32, shape, axis)


class _SeqLayout:
    def __init__(self, bsz, t_len, chunk_max):
        self.bsz, self.t_len = bsz, t_len
        blk = math.gcd(t_len, chunk_max)
        if blk % SUBLANES == 0:
            self.t_pad, self.chunk, self.valid = t_len, blk, blk
        else:
            assert blk == t_len and t_len < SUBLANES
            self.t_pad, self.chunk, self.valid = SUBLANES, SUBLANES, t_len
        self.tb = min(self.t_pad, 256)
        assert self.t_pad % self.tb == 0 and self.tb % self.chunk == 0
        self.nt = self.t_pad // self.tb
        self.n_chunks = self.tb // self.chunk
        self.m_pad = bsz * self.t_pad


def _row_mask(chunk, valid):
    return None if valid == chunk else (_iota2((chunk, 1), 0) < valid)


def _hgrn_kernel(q_ref, f_ref, i_ref, g_ref, s0_ref, lb_ref, nw_ref, o_ref, s_ref, st_ref, *, chunk, valid, n_chunks):
    hg = HEADS_PER_STEP
    t = pl.program_id(2)

    @pl.when(t == 0)
    def _():
        for h in range(hg):
            st_ref[h] = s0_ref[h].T

    lb = lb_ref[...]
    nw = nw_ref[...]
    row = _iota2((chunk, chunk), 0)
    col = _iota2((chunk, chunk), 1)
    causal = row >= col
    tri = causal.astype(BF16)
    mask = _row_mask(chunk, valid)

    def body(c, carry):
        r = pl.ds(pl.multiple_of(c * chunk, chunk), chunk)
        f = lb + (1.0 - lb) * _sigmoid(f_ref[r, :])
        log_f = jnp.log(f)
        k = 1.0 - f
        v = i_ref[r, :]
        if mask is not None:
            log_f = jnp.where(mask, log_f, 0.0)
            v = jnp.where(mask, v, 0.0)
        cum = _dot_select(tri, log_f)
        cum_last = cum[chunk - 1:chunk, :]
        q_dec = (_silu(q_ref[r, :]) * jnp.exp(cum)).astype(BF16)
        k_inv = (k * jnp.exp(-cum)).astype(BF16)
        k_end = (k * jnp.exp(cum_last - cum)).astype(BF16)
        g_end = jnp.exp(cum_last)
        vb = v.astype(BF16)
        gate = _silu(g_ref[r, :])
        for h in range(hg):
            sl = slice(h * DK_A, (h + 1) * DK_A)
            st = st_ref[h]
            scores = jnp.where(causal, _dot_nt(q_dec[:, sl], k_inv[:, sl]), 0.0)
            o = _dot(scores.astype(BF16), vb[:, sl]) + _dot_nt(q_dec[:, sl], st.astype(BF16))
            st_ref[h] = st * g_end[:, sl] + _dot_tn(vb[:, sl], k_end[:, sl])
            ms = jnp.mean(o * o, axis=-1, keepdims=True)
            o_ref[r, sl] = (o * lax.rsqrt(ms + RMS_EPS) * nw[:, sl] * gate[:, sl]).astype(o_ref.dtype)
        return carry

    lax.fori_loop(0, n_chunks, body, 0)

    @pl.when(t == pl.num_programs(2) - 1)
    def _():
        for h in range(hg):
            s_ref[h] = st_ref[h].T


def _hgrn(sq, u, s0, lb, norm_w):
    hg = HEADS_PER_STEP
    wl = hg * DK_A
    nsec = (H_A * DK_A) // wl

    def sec(s):
        return pl.BlockSpec((sq.tb, wl), lambda b, g, t: (b * sq.nt + t, s * nsec + g))

    state_spec = pl.BlockSpec((None, hg, DK_A, DV_A), lambda b, g, t: (b, g, 0, 0))
    vec_spec = pl.BlockSpec((1, wl), lambda b, g, t: (0, g))
    return pl.pallas_call(
        functools.partial(_hgrn_kernel, chunk=sq.chunk, valid=sq.valid, n_chunks=sq.n_chunks),
        grid=(sq.bsz, nsec, sq.nt),
        in_specs=[sec(0), sec(1), sec(2), sec(3), state_spec, vec_spec, vec_spec],
        out_specs=[pl.BlockSpec((sq.tb, wl), lambda b, g, t: (b * sq.nt + t, g)), state_spec],
        out_shape=[jax.ShapeDtypeStruct((sq.m_pad, H_A * DV_A), BF16),
                   jax.ShapeDtypeStruct((sq.bsz, H_A, DK_A, DV_A), F32)],
        scratch_shapes=[pltpu.VMEM((hg, DV_A, DK_A), F32)],
        compiler_params=_cparams(3),
        name="hgrn",
    )(u, u, u, u, s0, lb.reshape(1, -1), norm_w.reshape(1, -1))


def _delta_pre_kernel(qr_ref, kr_ref, vr_ref, cs_ref, w_ref, q_ref, k_ref, v_ref, carry_ref, *, tb):
    t = pl.program_id(1)
    width = H_B * DK_B

    @pl.when(t == 0)
    def _():
        for s in range(3):
            carry_ref[s] = cs_ref[:, s * width:(s + 1) * width]

    for s, (x_ref, y_ref) in enumerate(((qr_ref, q_ref), (kr_ref, k_ref), (vr_ref, v_ref))):
        w = w_ref[:, s * width:(s + 1) * width]
        x = x_ref[...]
        xx = jnp.concatenate([carry_ref[s], x], axis=0)
        y = w[CONV_W - 1:CONV_W] * x
        for j in range(1, CONV_W):
            y = y + w[CONV_W - 1 - j:CONV_W - j] * xx[SUBLANES - j:SUBLANES - j + tb]
        carry_ref[s] = x[tb - SUBLANES:tb]
        y = _silu(y)
        if s == 2:
            y_ref[...] = y
        else:
            scale = DK_B ** -0.5 if s == 0 else 1.0
            for h in range(H_B):
                sl = slice(h * DK_B, (h + 1) * DK_B)
                yh = y[:, sl]
                ss = jnp.sum(yh * yh, axis=-1, keepdims=True)
                y_ref[:, sl] = yh * (lax.rsqrt(ss + RMS_EPS) * scale)


def _delta_pre(sq, u, conv_state, conv_w):
    width = H_B * DK_B
    first = (4 * H_A * DK_A) // width

    def sec(s):
        return pl.BlockSpec((sq.tb, width), lambda b, t: (b * sq.nt + t, first + s))

    out_spec = pl.BlockSpec((sq.tb, width), lambda b, t: (b * sq.nt + t, 0))
    cs = jnp.pad(conv_state.astype(F32), ((0, 0), (SUBLANES - (CONV_W - 1), 0), (0, 0)))
    return pl.pallas_call(
        functools.partial(_delta_pre_kernel, tb=sq.tb),
        grid=(sq.bsz, sq.nt),
        in_specs=[sec(0), sec(1), sec(2),
                  pl.BlockSpec((None, SUBLANES, C_B), lambda b, t: (b, 0, 0)),
                  pl.BlockSpec((CONV_W, C_B), lambda b, t: (0, 0))],
        out_specs=[out_spec, out_spec, out_spec],
        out_shape=[jax.ShapeDtypeStruct((sq.m_pad, width), F32)] * 3,
        scratch_shapes=[pltpu.VMEM((3, SUBLANES, width), F32)],
        compiler_params=_cparams(2),
        name="delta_pre",
    )(u, u, u, cs, conv_w.astype(F32))


def _delta_kernel(q_ref, k_ref, v_ref, z_ref, gt_ref, s0_ref, alog_ref, dtb_ref, nw_ref, o_ref, s_ref,
                  wv_ref, wk_ref, qd_ref, ke_ref, qk_ref, u_ref, ge_ref, *, tb, chunk, valid, chunks_per_req, carried):
    hg = HEADS_PER_STEP
    n_chunks = tb // chunk
    shift = int(math.log2(chunk))
    assert 1 << shift == chunk

    if carried:
        @pl.when(pl.program_id(2) == 0)
        def _():
            s_ref[...] = s0_ref[...]
    else:
        s_ref[...] = s0_ref[...]

    neg_a = -jnp.exp(alog_ref[...])
    dtb = dtb_ref[...]
    nw = nw_ref[...]
    row = _iota2((tb, tb), 0)
    col = _iota2((tb, tb), 1)
    same = (row >> shift) == (col >> shift)
    incl = jnp.logical_and(same, row >= col)
    strict = jnp.logical_and(same, row > col)
    tri_bd = incl.astype(BF16)
    ones_bd = same.astype(BF16)
    eye = (row == col).astype(F32)
    pick = (_iota2((SUBLANES, V7X_LANES), 0) == _iota2((SUBLANES, V7X_LANES), 1)).astype(BF16)
    n_sq = max(shift - 1, 0)

    gt = gt_ref[...]
    log_g = neg_a * _softplus(gt + dtb)
    beta_all = _sigmoid(gt)
    if valid != chunk:
        real = (_iota2((tb, 1), 0) & (chunk - 1)) < valid
        log_g = jnp.where(real, log_g, 0.0)
        beta_all = jnp.where(real, beta_all, 0.0)
    cum = _dot_select(tri_bd, log_g)
    cum_tot = _dot_select(ones_bd, log_g)
    cum_t = _dot_select(pick, cum, nt=True)
    ge_ref[...] = jnp.exp(cum_tot)
    k_end_scale = jnp.exp(cum_tot - cum)
    e_all = jnp.exp(cum)

    for h in range(hg):
        sl = slice(h * DK_B, (h + 1) * DK_B)
        c_col = cum[:, h:h + 1]
        c_row = cum_t[h:h + 1, :]
        beta = beta_all[:, GATE_BETA_LANE + h:GATE_BETA_LANE + h + 1]
        decay = jnp.where(incl, jnp.exp(jnp.where(incl, c_col - c_row, 0.0)), 0.0)
        qh, kh, vh = q_ref[:, sl], k_ref[:, sl], v_ref[:, sl]
        qb, kb = qh.astype(BF16), kh.astype(BF16)
        nmat = jnp.where(strict, beta * _dot_nt(kb, kb) * decay, 0.0)
        inv = eye - nmat
        power = nmat
        for _ in range(n_sq):
            power = _dot_split(power, power)
            inv = inv + _dot_split(inv, power)
        e_col = e_all[:, h:h + 1]
        rhs = jnp.concatenate([beta * vh, (beta * e_col) * kh], axis=1)
        w = _dot_split(inv, rhs)
        wv_ref[h] = w[:, :DV_B]
        wk_ref[h] = w[:, DV_B:].astype(wk_ref.dtype)
        qd_ref[h] = (qh * e_col).astype(qd_ref.dtype)
        ke_ref[h] = (kh * k_end_scale[:, h:h + 1]).astype(ke_ref.dtype)
        qk_ref[h] = (_dot_nt(qb, kb) * decay).astype(qk_ref.dtype)
        u_ref[h] = jnp.zeros((tb, DV_B), u_ref.dtype)

    def body(j, carry):
        r = pl.ds(pl.multiple_of(j * chunk, chunk), chunk)
        req = j // chunks_per_req
        g_rows = ge_ref[r, :]
        for h in range(hg):
            sl = slice(h * DK_B, (h + 1) * DK_B)
            state = s_ref[req, h]
            sb = state.astype(BF16)
            both = _dot(jnp.concatenate([wk_ref[h, r, :], qd_ref[h, r, :]], axis=0).astype(BF16), sb)
            u = wv_ref[h, r, :] - both[:chunk]
            ub = u.astype(BF16)
            u_ref[h, r, :] = u.astype(u_ref.dtype)
            o = both[chunk:] + _dot(qk_ref[h, r, :].astype(BF16), u_ref[h].astype(BF16))
            s_ref[req, h] = g_rows[0:1, h:h + 1] * state + _dot_tn(ke_ref[h, r, :].astype(BF16), ub)
            ms = jnp.mean(o * o, axis=-1, keepdims=True)
            o_ref[r, sl] = (o * lax.rsqrt(ms + RMS_EPS) * nw * _silu(z_ref[r, sl])).astype(o_ref.dtype)
        return carry

    lax.fori_loop(0, n_chunks, body, 0, unroll=1 if carried else 4)


def _delta_gate_lanes(v, fill):
    hg = HEADS_PER_STEP
    v = v.astype(F32).reshape(H_B // hg, 1, hg)
    return jnp.pad(v, ((0, 0), (0, 0), (0, V7X_LANES - hg)), constant_values=fill)


def _delta_gate_weights(w_in0):
    hg = HEADS_PER_STEP
    ab = w_in0[:, IN0_MAIN:IN0_MAIN + H_B]
    bb = w_in0[:, IN0_MAIN + H_B:]
    tiles = []
    for g in range(H_B // hg):
        tile = jnp.zeros((D_MODEL, V7X_LANES), w_in0.dtype)
        tile = tile.at[:, :hg].set(ab[:, g * hg:(g + 1) * hg])
        tile = tile.at[:, GATE_BETA_LANE:GATE_BETA_LANE + hg].set(bb[:, g * hg:(g + 1) * hg])
        tiles.append(tile)
    return jnp.concatenate(tiles, axis=1).astype(BF16)


def _delta(sq, q, k, v, u, gates, s0, a_log, dt_bias, norm_w):
    hg = HEADS_PER_STEP
    wl = hg * DK_B
    ngrp = H_B // hg
    z_first = (4 * H_A * DK_A + C_B) // wl
    if sq.valid == sq.chunk:
        tb, reqs, nb, nt, carried = sq.tb, 1, sq.bsz, sq.nt, True
    else:
        assert sq.t_pad == sq.chunk
        reqs = math.gcd(sq.bsz, 128 // sq.t_pad)
        tb, nb, nt, carried = reqs * sq.t_pad, sq.bsz // reqs, 1, False
    chunks_per_req = (tb // sq.chunk) // reqs
    row_dt = BF16 if sq.chunk % (2 * SUBLANES) == 0 else F32

    def rows(first):
        return pl.BlockSpec((tb, wl), lambda b, g, t: (b * nt + t, first + g))

    state_spec = pl.BlockSpec((reqs, hg, DK_B, DV_B), lambda b, g, t: (b, g, 0, 0))
    lane_spec = pl.BlockSpec((None, 1, V7X_LANES), lambda b, g, t: (g, 0, 0))
    return pl.pallas_call(
        functools.partial(_delta_kernel, tb=tb, chunk=sq.chunk, valid=sq.valid, chunks_per_req=chunks_per_req,
                          carried=carried),
        grid=(nb, ngrp, nt),
        in_specs=[rows(0), rows(0), rows(0), rows(z_first),
                  pl.BlockSpec((tb, V7X_LANES), lambda b, g, t: (b * nt + t, g)),
                  state_spec, lane_spec, lane_spec,
                  pl.BlockSpec((1, DV_B), lambda b, g, t: (0, 0))],
        out_specs=[rows(0), state_spec],
        out_shape=[jax.ShapeDtypeStruct((sq.m_pad, H_B * DV_B), BF16),
                   jax.ShapeDtypeStruct((sq.bsz, H_B, DK_B, DV_B), F32)],
        scratch_shapes=[pltpu.VMEM((hg, tb, DV_B), F32), pltpu.VMEM((hg, tb, DK_B), row_dt),
                        pltpu.VMEM((hg, tb, DK_B), row_dt), pltpu.VMEM((hg, tb, DK_B), row_dt),
                        pltpu.VMEM((hg, tb, tb), row_dt), pltpu.VMEM((hg, tb, DV_B), row_dt),
                        pltpu.VMEM((tb, V7X_LANES), F32)],
        compiler_params=_cparams(3),
        name="delta",
    )(q, k, v, u, gates, s0, _delta_gate_lanes(a_log, 0.0), _delta_gate_lanes(dt_bias, 0.0),
      norm_w.astype(F32).reshape(1, DV_B))


def _rglru_kernel(x_ref, gate_ref, cs_ref, h0_ref, cw_ref, cb_ref, wa_ref, wx_ref, ba_ref, bx_ref, lam_ref,
                  y_ref, hl_ref, carry_ref, h_ref, a_ref, b_ref, *, tb, valid_rows, first_pos_is_zero):
    t = pl.program_id(1)

    @pl.when(t == 0)
    def _():
        carry_ref[...] = cs_ref[...]
        h_ref[...] = h0_ref[...]

    w = cw_ref[...]
    x = x_ref[...]
    xx = jnp.concatenate([carry_ref[...], x], axis=0)
    xr = w[CONV_W - 1:CONV_W] * x + cb_ref[...]
    for j in range(1, CONV_W):
        xr = xr + w[CONV_W - 1 - j:CONV_W - j] * xx[SUBLANES - j:SUBLANES - j + tb]
    carry_ref[...] = x[tb - SUBLANES:tb]
    neg_c_sp = -RG_C * _softplus(-lam_ref[...])
    for n in range(RG_BLOCKS):
        sl = slice(n * RG_BW, (n + 1) * RG_BW)
        xn = xr[:, sl]
        xb = xn.astype(BF16)
        r = _sigmoid(_dot(xb, wa_ref[n]) + ba_ref[:, sl])
        i = _sigmoid(_dot(xb, wx_ref[n]) + bx_ref[:, sl])
        log_a = neg_c_sp[:, sl] * r
        mult = jnp.sqrt(_neg_expm1(2.0 * log_a))
        if first_pos_is_zero:
            is_first = jnp.logical_and(t == 0, _iota2((tb, 1), 0) == 0)
            mult = jnp.where(is_first, 1.0, mult)
        a_ref[:, sl] = jnp.exp(log_a)
        b_ref[:, sl] = mult * i * xn

    def step(s, h):
        h = a_ref[pl.ds(s, 1), :] * h + b_ref[pl.ds(s, 1), :]
        b_ref[pl.ds(s, 1), :] = h
        return h

    h_ref[...] = lax.fori_loop(0, valid_rows, step, h_ref[...])
    y_ref[...] = (b_ref[...] * jax.nn.gelu(gate_ref[...], approximate=True)).astype(y_ref.dtype)

    @pl.when(t == pl.num_programs(1) - 1)
    def _():
        hl_ref[...] = h_ref[...]


def _rglru(sq, u, conv_state, h0, p, first_pos_is_zero):
    def sec(s):
        return pl.BlockSpec((sq.tb, RG_WIDTH), lambda b, t: (b * sq.nt + t, s))

    vec = pl.BlockSpec((1, RG_WIDTH), lambda b, t: (0, 0))
    blk = pl.BlockSpec((RG_BLOCKS, RG_BW, RG_BW), lambda b, t: (0, 0, 0))
    state = pl.BlockSpec((None, 1, RG_WIDTH), lambda b, t: (b, 0, 0))
    cs = jnp.pad(conv_state.astype(F32), ((0, 0), (SUBLANES - (CONV_W - 1), 0), (0, 0)))
    valid_rows = sq.tb if sq.valid == sq.chunk else sq.valid
    y, h_last = pl.pallas_call(
        functools.partial(_rglru_kernel, tb=sq.tb, valid_rows=valid_rows, first_pos_is_zero=first_pos_is_zero),
        grid=(sq.bsz, sq.nt),
        in_specs=[sec(0), sec(1),
                  pl.BlockSpec((None, SUBLANES, RG_WIDTH), lambda b, t: (b, 0, 0)), state,
                  pl.BlockSpec((CONV_W, RG_WIDTH), lambda b, t: (0, 0)), vec, blk, blk, vec, vec, vec],
        out_specs=[pl.BlockSpec((sq.tb, RG_WIDTH), lambda b, t: (b * sq.nt + t, 0)), state],
        out_shape=[jax.ShapeDtypeStruct((sq.m_pad, RG_WIDTH), BF16),
                   jax.ShapeDtypeStruct((sq.bsz, 1, RG_WIDTH), F32)],
        scratch_shapes=[pltpu.VMEM((SUBLANES, RG_WIDTH), F32), pltpu.VMEM((1, RG_WIDTH), F32),
                        pltpu.VMEM((sq.tb, RG_WIDTH), F32), pltpu.VMEM((sq.tb, RG_WIDTH), F32)],
        compiler_params=_cparams(2),
        name="rglru",
    )(u, u, cs, h0.astype(F32).reshape(sq.bsz, 1, RG_WIDTH), p['rg_conv_w'].astype(F32),
      p['rg_conv_b'].astype(F32).reshape(1, -1), p['rg_wa'].astype(BF16), p['rg_wx'].astype(BF16),
      p['rg_ba'].astype(F32).reshape(1, -1), p['rg_bx'].astype(F32).reshape(1, -1),
      p['rg_lambda'].astype(F32).reshape(1, -1))
    return y, h_last.reshape(sq.bsz, RG_WIDTH)


def _ret_kernel(q_ref, k_ref, v_ref, g_ref, cos_ref, sin_ref, s0_ref, o_ref, s_ref, *, chunk, valid, n_chunks):
    t = pl.program_id(1)

    @pl.when(t == 0)
    def _():
        s_ref[...] = s0_ref[...]

    row = _iota2((chunk, chunk), 0)
    col = _iota2((chunk, chunk), 1)
    causal = row >= col
    seen = jnp.minimum(_iota2((chunk, 1), 0) + 1, valid).astype(F32)
    dist = (row - col).astype(F32)
    mask = _row_mask(chunk, valid)
    log_gamma = [math.log1p(-2.0 ** (-5.0 - h)) for h in range(H_D)]

    def body(c, carry):
        r = pl.ds(pl.multiple_of(c * chunk, chunk), chunk)
        cos = cos_ref[r, :]
        sin = sin_ref[r, :]
        for h in range(H_D):
            lg = log_gamma[h]
            slk = slice(h * DK_D, (h + 1) * DK_D)
            slv = slice(h * DV_D, (h + 1) * DV_D)
            qh, kh = q_ref[r, slk], k_ref[r, slk]
            qr = qh * cos + pltpu.roll(qh, DK_D // 2, axis=1) * sin
            kr = (kh * cos + pltpu.roll(kh, DK_D // 2, axis=1) * sin) * (DK_D ** -0.5)
            vh = v_ref[r, slv]
            if mask is not None:
                vh = jnp.where(mask, vh, 0.0)
            vb = vh.astype(BF16)
            scores = jnp.where(causal, _dot_nt(qr.astype(BF16), kr.astype(BF16)) * jnp.exp(dist * lg), 0.0)
            state = s_ref[h]
            o = _dot(scores.astype(BF16), vb) + _dot((qr * jnp.exp(seen * lg)).astype(BF16), state.astype(BF16))
            k_end = (kr * jnp.exp((valid - seen) * lg)).astype(BF16)
            s_ref[h] = math.exp(valid * lg) * state + _dot_tn(k_end, vb)
            mu = jnp.mean(o, axis=-1, keepdims=True)
            oc = o - mu
            var = jnp.mean(oc * oc, axis=-1, keepdims=True)
            o_ref[r, slv] = (oc * lax.rsqrt(var + LN_EPS) * _silu(g_ref[r, slv])).astype(o_ref.dtype)
        return carry

    lax.fori_loop(0, n_chunks, body, 0)


def _rotary_tables(sq, start):
    half = DK_D // 2
    inv = ROPE_BASE ** (-jnp.arange(half, dtype=F32) / half)
    pos = start + jnp.arange(sq.t_pad, dtype=jnp.int32)
    ang = pos.astype(F32)[:, None] * inv
    cos, sin = jnp.cos(ang), jnp.sin(ang)
    cos2 = jnp.concatenate([cos, cos], axis=-1)
    sin2 = jnp.concatenate([-sin, sin], axis=-1)
    return cos2, sin2


def _retention(sq, u, s0, start):
    wq = H_D * DK_D
    wv = H_D * DV_D
    q_first = (2 * RG_WIDTH) // wq
    v_first = (2 * RG_WIDTH + 2 * wq) // wv

    def rows(width, first):
        return pl.BlockSpec((sq.tb, width), lambda b, t: (b * sq.nt + t, first))

    tab = pl.BlockSpec((sq.tb, DK_D), lambda b, t: (t, 0))
    state = pl.BlockSpec((None, H_D, DK_D, DV_D), lambda b, t: (b, 0, 0, 0))
    cos2, sin2 = _rotary_tables(sq, start)
    return pl.pallas_call(
        functools.partial(_ret_kernel, chunk=sq.chunk, valid=sq.valid, n_chunks=sq.n_chunks),
        grid=(sq.bsz, sq.nt),
        in_specs=[rows(wq, q_first), rows(wq, q_first + 1), rows(wv, v_first), rows(wv, v_first + 1), tab, tab, state],
        out_specs=[rows(wv, 0), state],
        out_shape=[jax.ShapeDtypeStruct((sq.m_pad, wv), BF16),
                   jax.ShapeDtypeStruct((sq.bsz, H_D, DK_D, DV_D), F32)],
        compiler_params=_cparams(2),
        name="retention",
    )(u, u, u, u, cos2, sin2, s0.astype(F32))


def _pad_rows(sq, a):
    if sq.t_pad == sq.t_len:
        return a
    a = a.reshape(sq.bsz, sq.t_len, a.shape[-1])
    return jnp.pad(a, ((0, 0), (0, sq.t_pad - sq.t_len), (0, 0))).reshape(sq.m_pad, a.shape[-1])


def _real_rows(sq, a):
    if sq.t_pad == sq.t_len:
        return a
    return a.reshape(sq.bsz, sq.t_pad, a.shape[-1])[:, :sq.t_len].reshape(sq.bsz * sq.t_len, a.shape[-1])


def _last_inputs(sq, u, first_col, width):
    assert sq.t_len >= CONV_W - 1
    u3 = u.reshape(sq.bsz, sq.t_pad, u.shape[-1])
    return u3[:, sq.t_len - (CONV_W - 1):sq.t_len, first_col:first_col + width]


def _mixer_ab(h, bsz, t_len, layer, s_hgrn, s_delta, s_dconv, p, wb):
    sq_a = _SeqLayout(bsz, t_len, CHUNK_A)
    sq_b = _SeqLayout(bsz, t_len, CHUNK_B)
    assert sq_a.t_pad == sq_b.t_pad
    hp = _pad_rows(sq_a, h)
    tm = min(sq_a.m_pad, 1024)
    u = _matmul(hp, wb['w_in0'], IN0_MAIN, F32, tm, 1024)
    gates = _matmul(hp, wb['w_in0_gates'], wb['w_in0_gates'].shape[1], F32, tm, wb['w_in0_gates'].shape[1])
    lb_all = jnp.cumsum(jax.nn.softmax(p['hgrn_lb_logits'].astype(F32), axis=0), axis=0)
    o_a, hgrn_new = _hgrn(sq_a, u, s_hgrn.astype(F32), lb_all[layer], p['hgrn_norm_w'].astype(F32))
    q, k, v = _delta_pre(sq_b, u, s_dconv, p['delta_conv_w'])
    o_b, delta_new = _delta(sq_b, q, k, v, u, gates, s_delta.astype(F32), p['delta_a_log'], p['delta_dt_bias'],
                            p['delta_norm_w'])
    dconv_new = _last_inputs(sq_b, u, 4 * H_A * DK_A, C_B)
    return _real_rows(sq_a, o_a), _real_rows(sq_b, o_b), (hgrn_new, delta_new, dconv_new)


def _mixer_cd(h, bsz, t_len, start, s_rg, s_rgconv, s_ret, p, wb):
    sq = _SeqLayout(bsz, t_len, CHUNK_D)
    hp = _pad_rows(sq, h)
    u = _matmul(hp, wb['w_in1'], IN1, F32, min(sq.m_pad, 1024), 1024)
    y_rg, rg_new = _rglru(sq, u, s_rgconv, s_rg, p, first_pos_is_zero=(start == 0))
    o_d, ret_new = _retention(sq, u, s_ret, start)
    rgconv_new = _last_inputs(sq, u, 0, RG_WIDTH)
    return _real_rows(sq, y_rg), _real_rows(sq, o_d), (rg_new, rgconv_new, ret_new)


def _trunk(x, mod_all, start, states, p, wb):
    s_hgrn, s_delta, s_dconv, s_rg, s_rgconv, s_ret = states
    bsz, t_len, _ = x.shape
    lay = _RowLayout(bsz, t_len)
    m = lay.m
    tm_mm = min(m, 1024)
    tm_ffn = min(m, 2048)
    tm_out = min(m, 512)
    x = x.reshape(m, D_MODEL)

    mods = [lay.mod_arg(mod_all[layer]) for layer in range(DEPTH)]
    h = _modulate(lay, x, mods[0], 0)
    for layer in range(DEPTH):
        lg, lb = p['ln_g'][layer], p['ln_b'][layer]
        mid = _ffn_in(h, wb['ffn_w_in'], layer, 0, tm_ffn)
        ffn = _matmul(mid, wb['ffn_w_out'], D_MODEL, F32, tm_out, 512, lead=(layer, 0))
        x, h = _post_norm(lay, x, ffn, mods[layer], 0, lg[0], lb[0], 0.5, (mods[layer], 1))
        if layer % 2 == 0:
            o_1, o_2, (s_hgrn, s_delta, s_dconv) = _mixer_ab(h, bsz, t_len, layer, s_hgrn, s_delta, s_dconv, p, wb)
            w_o = wb['w_out0']
        else:
            o_1, o_2, (s_rg, s_rgconv, s_ret) = _mixer_cd(h, bsz, t_len, start, s_rg, s_rgconv, s_ret, p, wb)
            w_o = wb['w_out1']
        mix = _matmul_halves(o_1, o_2, w_o, tm_mm, 1024)
        x, h = _post_norm(lay, x, mix, mods[layer], 1, lg[1], lb[1], 1.0, (mods[layer], 2))
        mid = _ffn_in(h, wb['ffn_w_in'], layer, 1, tm_ffn)
        ffn = _matmul(mid, wb['ffn_w_out'], D_MODEL, F32, tm_out, 512, lead=(layer, 1))
        nxt = (mods[layer + 1], 0) if layer + 1 < DEPTH else None
        x, h = _post_norm(lay, x, ffn, mods[layer], 2, lg[2], lb[2], 0.5, nxt)
    return x.reshape(bsz, t_len, D_MODEL), (s_hgrn, s_delta, s_dconv, s_rg, s_rgconv, s_ret)


def kernel(x_prompt, x_sample, c_prompt, c_sample, state_hgrn, state_delta, state_delta_conv, state_rglru, state_rglru_conv, state_ret, ada_w, ada_b, ln_g, ln_b, ffn_w_in, ffn_w_out, w_in0, w_out0, hgrn_lb_logits, hgrn_norm_w, delta_conv_w, delta_a_log, delta_dt_bias, delta_norm_w, w_in1, w_out1, rg_conv_w, rg_conv_b, rg_wa, rg_ba, rg_wx, rg_bx, rg_lambda):
    p = {
        'ln_g': ln_g, 'ln_b': ln_b, 'hgrn_lb_logits': hgrn_lb_logits, 'hgrn_norm_w': hgrn_norm_w,
        'delta_conv_w': delta_conv_w, 'delta_a_log': delta_a_log, 'delta_dt_bias': delta_dt_bias,
        'delta_norm_w': delta_norm_w, 'rg_conv_w': rg_conv_w, 'rg_conv_b': rg_conv_b, 'rg_wa': rg_wa,
        'rg_ba': rg_ba, 'rg_wx': rg_wx, 'rg_bx': rg_bx, 'rg_lambda': rg_lambda,
    }
    wb = {
        'ffn_w_in': ffn_w_in,
        'ffn_w_out': ffn_w_out.astype(BF16),
        'w_in0': w_in0.astype(BF16),
        'w_in0_gates': _delta_gate_weights(w_in0),
        'w_out0': w_out0.astype(BF16),
        'w_in1': w_in1.astype(BF16),
        'w_out1': w_out1.astype(BF16),
    }
    nb, ns = x_prompt.shape[0], x_sample.shape[0]
    c_all = jnp.concatenate([c_prompt, c_sample], axis=0).astype(F32)
    rows = nb + ns
    rows_pad = -(-rows // 16) * 16
    c_act = jnp.pad(jax.nn.silu(c_all), ((0, rows_pad - rows), (0, 0))).astype(BF16)
    mod = _ada(c_act, ada_w, ada_b)
    mod_prompt, mod_sample = mod[:, :nb], mod[:, nb:rows]

    prompt_states = (
        jnp.zeros((nb, H_A, DK_A, DV_A), F32),
        jnp.zeros((nb, H_B, DK_B, DV_B), F32),
        jnp.zeros((nb, CONV_W - 1, C_B), F32),
        jnp.zeros((nb, RG_WIDTH), F32),
        jnp.zeros((nb, CONV_W - 1, RG_WIDTH), F32),
        jnp.zeros((nb, H_D, DK_D, DV_D), F32),
    )
    y_prompt, ps = _trunk(x_prompt, mod_prompt, 0, prompt_states, p, wb)
    sample_states = (state_hgrn, state_delta, state_delta_conv, state_rglru, state_rglru_conv, state_ret)
    y_sample, ss = _trunk(x_sample, mod_sample, PAST_LEN, sample_states, p, wb)
    return (y_prompt, y_sample) + tuple(ps) + tuple(ss)
```

```python
import functools
import math

import jax
import jax.numpy as jnp
import numpy as np
from jax import lax
from jax.experimental import pallas as pl
from jax.experimental.pallas import tpu as pltpu

D_MODEL = 4096
DEPTH = 2
PAST_LEN = 16384
MIX_HALF = D_MODEL // 2
DK_A = 128
DV_A = 128
H_A = MIX_HALF // DV_A
DK_B = 128
DV_B = 128
H_B = MIX_HALF // DV_B
CONV_W = 4
C_B = 2 * H_B * DK_B + H_B * DV_B
RG_WIDTH = MIX_HALF
RG_BW = 128
RG_BLOCKS = RG_WIDTH // RG_BW
RG_C = 8.0
DK_D = 128
DV_D = 256
H_D = MIX_HALF // DV_D
D_FF = 11008
CHUNK_A = 16
CHUNK_B = 64
CHUNK_D = 64
ROPE_BASE = 10000.0
LN_EPS = 1e-5
RMS_EPS = 1e-6
ALPHA = (2 * DEPTH) ** 0.25

IN0_SIZES = (H_A * DK_A, H_A * DK_A, H_A * DV_A, H_A * DV_A, C_B, H_B * DV_B, H_B, H_B)
IN0 = sum(IN0_SIZES)
IN0_MAIN = IN0 - 2 * H_B
IN1_SIZES = (RG_WIDTH, RG_WIDTH, H_D * DK_D, H_D * DK_D, H_D * DV_D, H_D * DV_D)
IN1 = sum(IN1_SIZES)

F32 = jnp.float32
BF16 = jnp.bfloat16

V7X_VMEM_LIMIT_BYTES = 56 * 1024 * 1024
V7X_LANES = 128
SUBLANES = 8
FFN_TN = 256
HEADS_PER_STEP = 4
GATE_BETA_LANE = 16


def _cparams(n_axes):
    return pltpu.CompilerParams(dimension_semantics=("arbitrary",) * n_axes,
                                vmem_limit_bytes=V7X_VMEM_LIMIT_BYTES)


def _dot(a, b):
    return jnp.dot(a, b, preferred_element_type=F32)


def _dot_nt(a, b):
    return lax.dot_general(a, b, (((1,), (1,)), ((), ())), preferred_element_type=F32)


def _dot_tn(a, b):
    return lax.dot_general(a, b, (((0,), (0,)), ((), ())), preferred_element_type=F32)


def _split_bf16(x, terms):
    parts = []
    for _ in range(terms - 1):
        hi = x.astype(BF16)
        parts.append(hi)
        x = x - hi.astype(F32)
    parts.append(x.astype(BF16))
    return parts


def _dot_split(a, b):
    ah, al = _split_bf16(a, 2)
    bh, bl = _split_bf16(b, 2)
    return _dot(ah, bh) + (_dot(ah, bl) + _dot(al, bh))


def _dot_select(sel, x, nt=False):
    f = _dot_nt if nt else _dot
    x1, x2, x3 = _split_bf16(x, 3)
    return f(sel, x1) + (f(sel, x2) + f(sel, x3))


def _iota2(shape, axis):
    return lax.broadcasted_iota(jnp.int32, shape, axis)


def _mm_kernel(x_ref, w_ref, o_ref):
    o_ref[...] = jnp.dot(x_ref[...], w_ref[...], preferred_element_type=F32).astype(o_ref.dtype)


def _matmul(x, w, n_cols, out_dtype, tm, tn, lead=()):
    m, k = x.shape
    assert m % tm == 0 and n_cols % tn == 0 and w.shape[len(lead)] == k
    return pl.pallas_call(
        _mm_kernel,
        grid=(m // tm, n_cols // tn),
        in_specs=[pl.BlockSpec((tm, k), lambda i, j: (i, 0)),
                  pl.BlockSpec((None,) * len(lead) + (k, tn), lambda i, j: lead + (0, j))],
        out_specs=pl.BlockSpec((tm, tn), lambda i, j: (i, j)),
        out_shape=jax.ShapeDtypeStruct((m, n_cols), out_dtype),
        compiler_params=_cparams(2),
        name="matmul",
    )(x, w)


def _mm2_kernel(xa_ref, xb_ref, wa_ref, wb_ref, o_ref):
    o_ref[...] = (jnp.dot(xa_ref[...], wa_ref[...], preferred_element_type=F32)
                  + jnp.dot(xb_ref[...], wb_ref[...], preferred_element_type=F32)).astype(o_ref.dtype)


def _matmul_halves(xa, xb, w, tm, tn):
    m, kh = xa.shape
    n = w.shape[1]
    return pl.pallas_call(
        _mm2_kernel,
        grid=(m // tm, n // tn),
        in_specs=[pl.BlockSpec((tm, kh), lambda i, j: (i, 0)), pl.BlockSpec((tm, kh), lambda i, j: (i, 0)),
                  pl.BlockSpec((kh, tn), lambda i, j: (0, j)), pl.BlockSpec((kh, tn), lambda i, j: (1, j))],
        out_specs=pl.BlockSpec((tm, tn), lambda i, j: (i, j)),
        out_shape=jax.ShapeDtypeStruct((m, n), F32),
        compiler_params=_cparams(2),
        name="matmul_halves",
    )(xa, xb, w, w)


def _swiglu_kernel(x_ref, wg_ref, wu_ref, o_ref):
    x = x_ref[...]
    gate = jnp.dot(x, wg_ref[...].astype(BF16), preferred_element_type=F32)
    up = jnp.dot(x, wu_ref[...].astype(BF16), preferred_element_type=F32)
    o_ref[...] = (gate * jax.nn.sigmoid(gate) * up).astype(o_ref.dtype)


def _ffn_in(x, ffn_w_in, layer, half, tm):
    m, k = x.shape
    n_tiles = D_FF // FFN_TN
    return pl.pallas_call(
        _swiglu_kernel,
        grid=(m // tm, n_tiles),
        in_specs=[pl.BlockSpec((tm, k), lambda i, j: (i, 0), pipeline_mode=pl.Buffered(1)),
                  pl.BlockSpec((None, None, k, FFN_TN), lambda i, j: (layer, half, 0, j)),
                  pl.BlockSpec((None, None, k, FFN_TN), lambda i, j: (layer, half, 0, j + n_tiles))],
        out_specs=pl.BlockSpec((tm, FFN_TN), lambda i, j: (i, j)),
        out_shape=jax.ShapeDtypeStruct((m, D_FF), BF16),
        compiler_params=_cparams(2),
        name="ffn_in",
    )(x, ffn_w_in, ffn_w_in)


def _ada_kernel(c_ref, w_ref, b_ref, o_ref):
    o_ref[...] = jnp.dot(c_ref[...], w_ref[...].astype(BF16), preferred_element_type=F32) + b_ref[...]


def _ada(c_act, ada_w, ada_b, tn=512):
    r = c_act.shape[0]
    n = ada_w.shape[2]
    return pl.pallas_call(
        _ada_kernel,
        grid=(DEPTH, n // tn),
        in_specs=[pl.BlockSpec((r, D_MODEL), lambda l, j: (0, 0)),
                  pl.BlockSpec((None, D_MODEL, tn), lambda l, j: (l, 0, j)),
                  pl.BlockSpec((None, 1, tn), lambda l, j: (l, 0, j))],
        out_specs=pl.BlockSpec((None, r, tn), lambda l, j: (l, 0, j)),
        out_shape=jax.ShapeDtypeStruct((DEPTH, r, n), F32),
        compiler_params=_cparams(2),
        name="ada",
    )(c_act, ada_w, ada_b.reshape(DEPTH, 1, n))


def _per_row(mod_ref, rows, rep):
    v = mod_ref[...]
    if rep == 1:
        return v
    sel = (_iota2((rows, rows // rep), 0) // rep == _iota2((rows, rows // rep), 1)).astype(BF16)
    return _dot_select(sel, v)


def _modulate_kernel(x_ref, shift_ref, scale_ref, h_ref, *, rep):
    rows = x_ref.shape[0]
    h_ref[...] = (x_ref[...] * (1.0 + _per_row(scale_ref, rows, rep)) + _per_row(shift_ref, rows, rep)).astype(h_ref.dtype)


def _post_norm_body(x_ref, out_ref, gate_ref, g_ref, b_ref, rho, rep):
    z = ALPHA * x_ref[...] + rho * _per_row(gate_ref, x_ref.shape[0], rep) * out_ref[...]
    mu = jnp.mean(z, axis=-1, keepdims=True)
    zc = z - mu
    var = jnp.mean(zc * zc, axis=-1, keepdims=True)
    return zc * lax.rsqrt(var + LN_EPS) * g_ref[...] + b_ref[...]


def _post_norm_mod_kernel(x_ref, out_ref, gate_ref, g_ref, b_ref, shift_ref, scale_ref, xn_ref, h_ref, *, rho, rep):
    rows = x_ref.shape[0]
    xn = _post_norm_body(x_ref, out_ref, gate_ref, g_ref, b_ref, rho, rep)
    xn_ref[...] = xn
    h_ref[...] = (xn * (1.0 + _per_row(scale_ref, rows, rep)) + _per_row(shift_ref, rows, rep)).astype(h_ref.dtype)


def _post_norm_kernel(x_ref, out_ref, gate_ref, g_ref, b_ref, xn_ref, *, rho, rep):
    xn_ref[...] = _post_norm_body(x_ref, out_ref, gate_ref, g_ref, b_ref, rho, rep)


class _RowLayout:
    def __init__(self, bsz, t_len):
        self.bsz, self.t_len = bsz, t_len
        self.m = bsz * t_len
        self.per_tile = t_len >= 256
        self.tm = min(self.m, 128)
        self.rep = 1 if self.per_tile else t_len
        assert self.per_tile or (self.tm % t_len == 0 and (self.tm // t_len) % SUBLANES == 0)

    def mod_arg(self, mod_layer):
        if self.per_tile:
            return mod_layer.reshape(self.bsz, 1, 9 * D_MODEL)
        return mod_layer

    def mod_spec(self, col):
        if self.per_tile:
            per = self.t_len // self.tm
            return pl.BlockSpec((None, 1, D_MODEL), lambda i: (i // per, 0, col))
        return pl.BlockSpec((self.tm // self.t_len, D_MODEL), lambda i: (i, col))

    def row_spec(self):
        return pl.BlockSpec((self.tm, D_MODEL), lambda i: (i, 0))

    @staticmethod
    def vec_spec():
        return pl.BlockSpec((1, D_MODEL), lambda i: (0, 0))


def _modulate(lay, x, mods, sub):
    return pl.pallas_call(
        functools.partial(_modulate_kernel, rep=lay.rep),
        grid=(lay.m // lay.tm,),
        in_specs=[lay.row_spec(), lay.mod_spec(3 * sub), lay.mod_spec(3 * sub + 1)],
        out_specs=lay.row_spec(),
        out_shape=jax.ShapeDtypeStruct((lay.m, D_MODEL), BF16),
        compiler_params=_cparams(1),
        name="modulate",
    )(x, mods, mods)


def _post_norm(lay, x, out, mods, sub, g, b, rho, nxt):
    g2, b2 = g.reshape(1, D_MODEL), b.reshape(1, D_MODEL)
    common = dict(grid=(lay.m // lay.tm,), compiler_params=_cparams(1))
    base_specs = [lay.row_spec(), lay.row_spec(), lay.mod_spec(3 * sub + 2), lay.vec_spec(), lay.vec_spec()]
    if nxt is None:
        return pl.pallas_call(
            functools.partial(_post_norm_kernel, rho=rho, rep=lay.rep),
            in_specs=base_specs,
            out_specs=lay.row_spec(),
            out_shape=jax.ShapeDtypeStruct((lay.m, D_MODEL), F32),
            name="post_norm", **common,
        )(x, out, mods, g2, b2), None
    nxt_mods, nxt_sub = nxt
    return pl.pallas_call(
        functools.partial(_post_norm_mod_kernel, rho=rho, rep=lay.rep),
        in_specs=base_specs + [lay.mod_spec(3 * nxt_sub), lay.mod_spec(3 * nxt_sub + 1)],
        out_specs=[lay.row_spec(), lay.row_spec()],
        out_shape=[jax.ShapeDtypeStruct((lay.m, D_MODEL), F32), jax.ShapeDtypeStruct((lay.m, D_MODEL), BF16)],
        name="post_norm_mod", **common,
    )(x, out, mods, g2, b2, nxt_mods, nxt_mods)


def _sigmoid(x):
    return 1.0 / (1.0 + jnp.exp(-x))


def _silu(x):
    return x * _sigmoid(x)


def _softplus(x):
    return jnp.maximum(x, 0.0) + jnp.log1p(jnp.exp(-jnp.abs(x)))


def _neg_expm1(x):
    return -jnp.tanh(0.5 * x) * (jnp.exp(x) + 1.0)


class _SeqLayout:
    def __init__(self, bsz, t_len, chunk_max):
        self.bsz, self.t_len = bsz, t_len
        blk = math.gcd(t_len, chunk_max)
        if blk % SUBLANES == 0:
            self.t_pad, self.chunk, self.valid = t_len, blk, blk
        else:
            assert blk == t_len and t_len < SUBLANES
            self.t_pad, self.chunk, self.valid = SUBLANES, SUBLANES, t_len
        self.tb = min(self.t_pad, 256)
        assert self.t_pad % self.tb == 0 and self.tb % self.chunk == 0
        self.nt = self.t_pad // self.tb
        self.n_chunks = self.tb // self.chunk
        self.m_pad = bsz * self.t_pad


def _row_mask(chunk, valid):
    return None if valid == chunk else (_iota2((chunk, 1), 0) < valid)


def _hgrn_kernel(q_ref, f_ref, i_ref, g_ref, s0_ref, lb_ref, nw_ref, o_ref, s_ref,
                 oi_ref, qd_ref, ke_ref, vb_ref, ge_ref, st_ref, *, tb, chunk, valid, chunks_per_req, carried):
    hg = HEADS_PER_STEP
    n_chunks = tb // chunk
    shift = int(math.log2(chunk))
    assert 1 << shift == chunk

    if carried:
        @pl.when(pl.program_id(2) == 0)
        def _():
            for h in range(hg):
                st_ref[h] = s0_ref[0, h].T
    else:
        s_ref[...] = s0_ref[...]

    lb = lb_ref[...]
    nw = nw_ref[...]
    row = _iota2((tb, tb), 0)
    col = _iota2((tb, tb), 1)
    same = (row >> shift) == (col >> shift)
    incl = jnp.logical_and(same, row >= col)

    f = lb + (1.0 - lb) * _sigmoid(f_ref[...])
    log_f = jnp.log(f)
    k = 1.0 - f
    v = i_ref[...]
    if valid != chunk:
        real = (_iota2((tb, 1), 0) & (chunk - 1)) < valid
        log_f = jnp.where(real, log_f, 0.0)
        v = jnp.where(real, v, 0.0)
    cum = _dot_select(incl.astype(BF16), log_f)
    cum_tot = _dot_select(same.astype(BF16), log_f)
    g_end = jnp.exp(cum_tot)
    ge_ref[...] = g_end
    q_dec = (_silu(q_ref[...]) * jnp.exp(cum)).astype(BF16)
    k_inv = (k * jnp.exp(-cum)).astype(BF16)
    k_end = k * jnp.exp(cum_tot - cum)
    vb = v.astype(BF16)
    eye_dk = (_iota2((DK_A, DK_A), 0) == _iota2((DK_A, DK_A), 1)).astype(BF16)
    g_cols = []
    for h in range(hg):
        sl = slice(h * DK_A, (h + 1) * DK_A)
        scores = jnp.where(incl, _dot_nt(q_dec[:, sl], k_inv[:, sl]), 0.0)
        oi_ref[h] = _dot(scores.astype(BF16), vb[:, sl])
        qd_ref[h] = q_dec[:, sl].astype(qd_ref.dtype)
        ke_ref[h] = k_end[:, sl].astype(ke_ref.dtype)
        vb_ref[h] = v[:, sl].astype(vb_ref.dtype)
        if not carried:
            g_cols.append(_dot_select(eye_dk, g_end[:, sl], nt=True))

    def finish(h, r, o):
        sl = slice(h * DK_A, (h + 1) * DK_A)
        ms = jnp.mean(o * o, axis=-1, keepdims=True)
        o_ref[r, sl] = (o * lax.rsqrt(ms + RMS_EPS) * nw[:, sl] * _silu(g_ref[r, sl])).astype(o_ref.dtype)

    if carried:
        def body(j, carry):
            r = pl.ds(pl.multiple_of(j * chunk, chunk), chunk)
            g_rows = ge_ref[r, :]
            for h in range(hg):
                sl = slice(h * DK_A, (h + 1) * DK_A)
                st = st_ref[h]
                o = oi_ref[h, r, :] + _dot_nt(qd_ref[h, r, :].astype(BF16), st.astype(BF16))
                st_ref[h] = st * g_rows[0:1, sl] + _dot_tn(vb_ref[h, r, :].astype(BF16), ke_ref[h, r, :].astype(BF16))
                finish(h, r, o)
            return carry

        lax.fori_loop(0, n_chunks, body, 0, unroll=4)

        @pl.when(pl.program_id(2) == pl.num_programs(2) - 1)
        def _():
            for h in range(hg):
                s_ref[0, h] = st_ref[h].T
    else:
        for j in range(n_chunks):
            r = slice(j * chunk, (j + 1) * chunk)
            req = j // chunks_per_req
            for h in range(hg):
                state = s_ref[req, h]
                o = oi_ref[h, r, :] + _dot(qd_ref[h, r, :].astype(BF16), state.astype(BF16))
                s_ref[req, h] = (g_cols[h][:, j * chunk:j * chunk + 1] * state
                                 + _dot_tn(ke_ref[h, r, :].astype(BF16), vb_ref[h, r, :].astype(BF16)))
                finish(h, r, o)


def _hgrn(sq, u, s0, lb, norm_w):
    hg = HEADS_PER_STEP
    wl = hg * DK_A
    nsec = (H_A * DK_A) // wl
    if sq.valid == sq.chunk:
        tb, reqs, nb, nt, carried = sq.tb, 1, sq.bsz, sq.nt, True
    else:
        assert sq.t_pad == sq.chunk
        reqs = math.gcd(sq.bsz, 128 // sq.t_pad)
        tb, nb, nt, carried = reqs * sq.t_pad, sq.bsz // reqs, 1, False
    chunks_per_req = (tb // sq.chunk) // reqs
    row_dt = BF16 if sq.chunk % (2 * SUBLANES) == 0 else F32

    def sec(s):
        return pl.BlockSpec((tb, wl), lambda b, g, t: (b * nt + t, s * nsec + g))

    state_spec = pl.BlockSpec((reqs, hg, DK_A, DV_A), lambda b, g, t: (b, g, 0, 0))
    vec_spec = pl.BlockSpec((1, wl), lambda b, g, t: (0, g))
    return pl.pallas_call(
        functools.partial(_hgrn_kernel, tb=tb, chunk=sq.chunk, valid=sq.valid, chunks_per_req=chunks_per_req,
                          carried=carried),
        grid=(nb, nsec, nt),
        in_specs=[sec(0), sec(1), sec(2), sec(3), state_spec, vec_spec, vec_spec],
        out_specs=[pl.BlockSpec((tb, wl), lambda b, g, t: (b * nt + t, g)), state_spec],
        out_shape=[jax.ShapeDtypeStruct((sq.m_pad, H_A * DV_A), BF16),
                   jax.ShapeDtypeStruct((sq.bsz, H_A, DK_A, DV_A), F32)],
        scratch_shapes=[pltpu.VMEM((hg, tb, DV_A), F32), pltpu.VMEM((hg, tb, DK_A), row_dt),
                        pltpu.VMEM((hg, tb, DK_A), row_dt), pltpu.VMEM((hg, tb, DV_A), row_dt),
                        pltpu.VMEM((tb, wl), F32), pltpu.VMEM((hg, DV_A, DK_A), F32)],
        compiler_params=_cparams(3),
        name="hgrn",
    )(u, u, u, u, s0, lb.reshape(1, -1), norm_w.reshape(1, -1))


def _delta_pre_kernel(qr_ref, kr_ref, vr_ref, cs_ref, w_ref, q_ref, k_ref, v_ref, carry_ref, *, tb):
    t = pl.program_id(1)
    width = H_B * DK_B

    @pl.when(t == 0)
    def _():
        for s in range(3):
            carry_ref[s] = cs_ref[:, s * width:(s + 1) * width]

    for s, (x_ref, y_ref) in enumerate(((qr_ref, q_ref), (kr_ref, k_ref), (vr_ref, v_ref))):
        w = w_ref[:, s * width:(s + 1) * width]
        x = x_ref[...]
        xx = jnp.concatenate([carry_ref[s], x], axis=0)
        y = w[CONV_W - 1:CONV_W] * x
        for j in range(1, CONV_W):
            y = y + w[CONV_W - 1 - j:CONV_W - j] * xx[SUBLANES - j:SUBLANES - j + tb]
        carry_ref[s] = x[tb - SUBLANES:tb]
        y = _silu(y)
        if s == 2:
            y_ref[...] = y
        else:
            scale = DK_B ** -0.5 if s == 0 else 1.0
            for h in range(H_B):
                sl = slice(h * DK_B, (h + 1) * DK_B)
                yh = y[:, sl]
                ss = jnp.sum(yh * yh, axis=-1, keepdims=True)
                y_ref[:, sl] = yh * (lax.rsqrt(ss + RMS_EPS) * scale)


def _delta_pre(sq, u, conv_state, conv_w):
    width = H_B * DK_B
    first = (4 * H_A * DK_A) // width

    def sec(s):
        return pl.BlockSpec((sq.tb, width), lambda b, t: (b * sq.nt + t, first + s))

    out_spec = pl.BlockSpec((sq.tb, width), lambda b, t: (b * sq.nt + t, 0))
    cs = jnp.pad(conv_state.astype(F32), ((0, 0), (SUBLANES - (CONV_W - 1), 0), (0, 0)))
    return pl.pallas_call(
        functools.partial(_delta_pre_kernel, tb=sq.tb),
        grid=(sq.bsz, sq.nt),
        in_specs=[sec(0), sec(1), sec(2),
                  pl.BlockSpec((None, SUBLANES, C_B), lambda b, t: (b, 0, 0)),
                  pl.BlockSpec((CONV_W, C_B), lambda b, t: (0, 0))],
        out_specs=[out_spec, out_spec, out_spec],
        out_shape=[jax.ShapeDtypeStruct((sq.m_pad, width), F32)] * 3,
        scratch_shapes=[pltpu.VMEM((3, SUBLANES, width), F32)],
        compiler_params=_cparams(2),
        name="delta_pre",
    )(u, u, u, cs, conv_w.astype(F32))


def _delta_kernel(q_ref, k_ref, v_ref, z_ref, gt_ref, s0_ref, alog_ref, dtb_ref, nw_ref, o_ref, s_ref,
                  wv_ref, wk_ref, qd_ref, ke_ref, qk_ref, u_ref, ge_ref, *, tb, chunk, valid, chunks_per_req, carried):
    hg = HEADS_PER_STEP
    n_chunks = tb // chunk
    shift = int(math.log2(chunk))
    assert 1 << shift == chunk

    if carried:
        @pl.when(pl.program_id(2) == 0)
        def _():
            s_ref[...] = s0_ref[...]
    else:
        s_ref[...] = s0_ref[...]

    neg_a = -jnp.exp(alog_ref[...])
    dtb = dtb_ref[...]
    nw = nw_ref[...]
    row = _iota2((tb, tb), 0)
    col = _iota2((tb, tb), 1)
    same = (row >> shift) == (col >> shift)
    incl = jnp.logical_and(same, row >= col)
    strict = jnp.logical_and(same, row > col)
    tri_bd = incl.astype(BF16)
    ones_bd = same.astype(BF16)
    eye = (row == col).astype(F32)
    pick = (_iota2((SUBLANES, V7X_LANES), 0) == _iota2((SUBLANES, V7X_LANES), 1)).astype(BF16)
    n_sq = max(shift - 1, 0)

    gt = gt_ref[...]
    log_g = neg_a * _softplus(gt + dtb)
    beta_all = _sigmoid(gt)
    if valid != chunk:
        real = (_iota2((tb, 1), 0) & (chunk - 1)) < valid
        log_g = jnp.where(real, log_g, 0.0)
        beta_all = jnp.where(real, beta_all, 0.0)
    cum = _dot_select(tri_bd, log_g)
    cum_tot = _dot_select(ones_bd, log_g)
    cum_t = _dot_select(pick, cum, nt=True)
    ge_ref[...] = jnp.exp(cum_tot)
    k_end_scale = jnp.exp(cum_tot - cum)
    e_all = jnp.exp(cum)

    for h in range(hg):
        sl = slice(h * DK_B, (h + 1) * DK_B)
        c_col = cum[:, h:h + 1]
        c_row = cum_t[h:h + 1, :]
        beta = beta_all[:, GATE_BETA_LANE + h:GATE_BETA_LANE + h + 1]
        decay = jnp.where(incl, jnp.exp(jnp.where(incl, c_col - c_row, 0.0)), 0.0)
        qh, kh, vh = q_ref[:, sl], k_ref[:, sl], v_ref[:, sl]
        qb, kb = qh.astype(BF16), kh.astype(BF16)
        nmat = jnp.where(strict, beta * _dot_nt(kb, kb) * decay, 0.0)
        inv = eye - nmat
        power = nmat
        for _ in range(n_sq):
            power = _dot_split(power, power)
            inv = inv + _dot_split(inv, power)
        e_col = e_all[:, h:h + 1]
        rhs = jnp.concatenate([beta * vh, (beta * e_col) * kh], axis=1)
        w = _dot_split(inv, rhs)
        wv_ref[h] = w[:, :DV_B]
        wk_ref[h] = w[:, DV_B:].astype(wk_ref.dtype)
        qd_ref[h] = (qh * e_col).astype(qd_ref.dtype)
        ke_ref[h] = (kh * k_end_scale[:, h:h + 1]).astype(ke_ref.dtype)
        qk_ref[h] = (_dot_nt(qb, kb) * decay).astype(qk_ref.dtype)
        u_ref[h] = jnp.zeros((tb, DV_B), u_ref.dtype)

    def body(j, carry):
        r = pl.ds(pl.multiple_of(j * chunk, chunk), chunk)
        req = j // chunks_per_req
        g_rows = ge_ref[r, :]
        for h in range(hg):
            sl = slice(h * DK_B, (h + 1) * DK_B)
            state = s_ref[req, h]
            sb = state.astype(BF16)
            both = _dot(jnp.concatenate([wk_ref[h, r, :], qd_ref[h, r, :]], axis=0).astype(BF16), sb)
            u = wv_ref[h, r, :] - both[:chunk]
            ub = u.astype(BF16)
            u_ref[h, r, :] = u.astype(u_ref.dtype)
            o = both[chunk:] + _dot(qk_ref[h, r, :].astype(BF16), u_ref[h].astype(BF16))
            s_ref[req, h] = g_rows[0:1, h:h + 1] * state + _dot_tn(ke_ref[h, r, :].astype(BF16), ub)
            ms = jnp.mean(o * o, axis=-1, keepdims=True)
            o_ref[r, sl] = (o * lax.rsqrt(ms + RMS_EPS) * nw * _silu(z_ref[r, sl])).astype(o_ref.dtype)
        return carry

    lax.fori_loop(0, n_chunks, body, 0, unroll=1 if carried else 4)


def _delta_gate_lanes(v, fill):
    hg = HEADS_PER_STEP
    v = v.astype(F32).reshape(H_B // hg, 1, hg)
    return jnp.pad(v, ((0, 0), (0, 0), (0, V7X_LANES - hg)), constant_values=fill)


def _delta_gate_weights(w_in0):
    hg = HEADS_PER_STEP
    ab = w_in0[:, IN0_MAIN:IN0_MAIN + H_B]
    bb = w_in0[:, IN0_MAIN + H_B:]
    tiles = []
    for g in range(H_B // hg):
        tile = jnp.zeros((D_MODEL, V7X_LANES), w_in0.dtype)
        tile = tile.at[:, :hg].set(ab[:, g * hg:(g + 1) * hg])
        tile = tile.at[:, GATE_BETA_LANE:GATE_BETA_LANE + hg].set(bb[:, g * hg:(g + 1) * hg])
        tiles.append(tile)
    return jnp.concatenate(tiles, axis=1).astype(BF16)


def _delta(sq, q, k, v, u, gates, s0, a_log, dt_bias, norm_w):
    hg = HEADS_PER_STEP
    wl = hg * DK_B
    ngrp = H_B // hg
    z_first = (4 * H_A * DK_A + C_B) // wl
    if sq.valid == sq.chunk:
        tb, reqs, nb, nt, carried = sq.tb, 1, sq.bsz, sq.nt, True
    else:
        assert sq.t_pad == sq.chunk
        reqs = math.gcd(sq.bsz, 128 // sq.t_pad)
        tb, nb, nt, carried = reqs * sq.t_pad, sq.bsz // reqs, 1, False
    chunks_per_req = (tb // sq.chunk) // reqs
    row_dt = BF16 if sq.chunk % (2 * SUBLANES) == 0 else F32

    def rows(first):
        return pl.BlockSpec((tb, wl), lambda b, g, t: (b * nt + t, first + g))

    state_spec = pl.BlockSpec((reqs, hg, DK_B, DV_B), lambda b, g, t: (b, g, 0, 0))
    lane_spec = pl.BlockSpec((None, 1, V7X_LANES), lambda b, g, t: (g, 0, 0))
    return pl.pallas_call(
        functools.partial(_delta_kernel, tb=tb, chunk=sq.chunk, valid=sq.valid, chunks_per_req=chunks_per_req,
                          carried=carried),
        grid=(nb, ngrp, nt),
        in_specs=[rows(0), rows(0), rows(0), rows(z_first),
                  pl.BlockSpec((tb, V7X_LANES), lambda b, g, t: (b * nt + t, g)),
                  state_spec, lane_spec, lane_spec,
                  pl.BlockSpec((1, DV_B), lambda b, g, t: (0, 0))],
        out_specs=[rows(0), state_spec],
        out_shape=[jax.ShapeDtypeStruct((sq.m_pad, H_B * DV_B), BF16),
                   jax.ShapeDtypeStruct((sq.bsz, H_B, DK_B, DV_B), F32)],
        scratch_shapes=[pltpu.VMEM((hg, tb, DV_B), F32), pltpu.VMEM((hg, tb, DK_B), row_dt),
                        pltpu.VMEM((hg, tb, DK_B), row_dt), pltpu.VMEM((hg, tb, DK_B), row_dt),
                        pltpu.VMEM((hg, tb, tb), row_dt), pltpu.VMEM((hg, tb, DV_B), row_dt),
                        pltpu.VMEM((tb, V7X_LANES), F32)],
        compiler_params=_cparams(3),
        name="delta",
    )(q, k, v, u, gates, s0, _delta_gate_lanes(a_log, 0.0), _delta_gate_lanes(dt_bias, 0.0),
      norm_w.astype(F32).reshape(1, DV_B))


def _rglru_kernel(x_ref, gate_ref, cs_ref, h0_ref, cw_ref, cb_ref, wa_ref, wx_ref, ba_ref, bx_ref, lam_ref,
                  y_ref, hl_ref, carry_ref, h_ref, a_ref, b_ref, *, tb, valid_rows, first_pos_is_zero):
    t = pl.program_id(1)

    @pl.when(t == 0)
    def _():
        carry_ref[...] = cs_ref[...]
        h_ref[...] = h0_ref[...]

    w = cw_ref[...]
    x = x_ref[...]
    xx = jnp.concatenate([carry_ref[...], x], axis=0)
    xr = w[CONV_W - 1:CONV_W] * x + cb_ref[...]
    for j in range(1, CONV_W):
        xr = xr + w[CONV_W - 1 - j:CONV_W - j] * xx[SUBLANES - j:SUBLANES - j + tb]
    carry_ref[...] = x[tb - SUBLANES:tb]
    neg_c_sp = -RG_C * _softplus(-lam_ref[...])
    for n in range(RG_BLOCKS):
        sl = slice(n * RG_BW, (n + 1) * RG_BW)
        xn = xr[:, sl]
        xb = xn.astype(BF16)
        r = _sigmoid(_dot(xb, wa_ref[n]) + ba_ref[:, sl])
        i = _sigmoid(_dot(xb, wx_ref[n]) + bx_ref[:, sl])
        log_a = neg_c_sp[:, sl] * r
        mult = jnp.sqrt(_neg_expm1(2.0 * log_a))
        if first_pos_is_zero:
            is_first = jnp.logical_and(t == 0, _iota2((tb, 1), 0) == 0)
            mult = jnp.where(is_first, 1.0, mult)
        a_ref[:, sl] = jnp.exp(log_a)
        b_ref[:, sl] = mult * i * xn

    def step(s, h):
        h = a_ref[pl.ds(s, 1), :] * h + b_ref[pl.ds(s, 1), :]
        b_ref[pl.ds(s, 1), :] = h
        return h

    h_ref[...] = lax.fori_loop(0, valid_rows, step, h_ref[...])
    y_ref[...] = (b_ref[...] * jax.nn.gelu(gate_ref[...], approximate=True)).astype(y_ref.dtype)

    @pl.when(t == pl.num_programs(1) - 1)
    def _():
        hl_ref[...] = h_ref[...]


def _rglru(sq, u, conv_state, h0, p, first_pos_is_zero):
    def sec(s):
        return pl.BlockSpec((sq.tb, RG_WIDTH), lambda b, t: (b * sq.nt + t, s))

    vec = pl.BlockSpec((1, RG_WIDTH), lambda b, t: (0, 0))
    blk = pl.BlockSpec((RG_BLOCKS, RG_BW, RG_BW), lambda b, t: (0, 0, 0))
    state = pl.BlockSpec((None, 1, RG_WIDTH), lambda b, t: (b, 0, 0))
    cs = jnp.pad(conv_state.astype(F32), ((0, 0), (SUBLANES - (CONV_W - 1), 0), (0, 0)))
    valid_rows = sq.tb if sq.valid == sq.chunk else sq.valid
    y, h_last = pl.pallas_call(
        functools.partial(_rglru_kernel, tb=sq.tb, valid_rows=valid_rows, first_pos_is_zero=first_pos_is_zero),
        grid=(sq.bsz, sq.nt),
        in_specs=[sec(0), sec(1),
                  pl.BlockSpec((None, SUBLANES, RG_WIDTH), lambda b, t: (b, 0, 0)), state,
                  pl.BlockSpec((CONV_W, RG_WIDTH), lambda b, t: (0, 0)), vec, blk, blk, vec, vec, vec],
        out_specs=[pl.BlockSpec((sq.tb, RG_WIDTH), lambda b, t: (b * sq.nt + t, 0)), state],
        out_shape=[jax.ShapeDtypeStruct((sq.m_pad, RG_WIDTH), BF16),
                   jax.ShapeDtypeStruct((sq.bsz, 1, RG_WIDTH), F32)],
        scratch_shapes=[pltpu.VMEM((SUBLANES, RG_WIDTH), F32), pltpu.VMEM((1, RG_WIDTH), F32),
                        pltpu.VMEM((sq.tb, RG_WIDTH), F32), pltpu.VMEM((sq.tb, RG_WIDTH), F32)],
        compiler_params=_cparams(2),
        name="rglru",
    )(u, u, cs, h0.astype(F32).reshape(sq.bsz, 1, RG_WIDTH), p['rg_conv_w'].astype(F32),
      p['rg_conv_b'].astype(F32).reshape(1, -1), p['rg_wa'].astype(BF16), p['rg_wx'].astype(BF16),
      p['rg_ba'].astype(F32).reshape(1, -1), p['rg_bx'].astype(F32).reshape(1, -1),
      p['rg_lambda'].astype(F32).reshape(1, -1))
    return y, h_last.reshape(sq.bsz, RG_WIDTH)


def _ret_kernel(q_ref, k_ref, v_ref, g_ref, cos_ref, sin_ref, s0_ref, o_ref, s_ref, *, chunk, valid, n_chunks):
    t = pl.program_id(1)

    @pl.when(t == 0)
    def _():
        s_ref[...] = s0_ref[...]

    row = _iota2((chunk, chunk), 0)
    col = _iota2((chunk, chunk), 1)
    causal = row >= col
    seen = jnp.minimum(_iota2((chunk, 1), 0) + 1, valid).astype(F32)
    dist = (row - col).astype(F32)
    mask = _row_mask(chunk, valid)
    log_gamma = [math.log1p(-2.0 ** (-5.0 - h)) for h in range(H_D)]

    def body(c, carry):
        r = pl.ds(pl.multiple_of(c * chunk, chunk), chunk)
        cos = cos_ref[r, :]
        sin = sin_ref[r, :]
        for h in range(H_D):
            lg = log_gamma[h]
            slk = slice(h * DK_D, (h + 1) * DK_D)
            slv = slice(h * DV_D, (h + 1) * DV_D)
            qh, kh = q_ref[r, slk], k_ref[r, slk]
            qr = qh * cos + pltpu.roll(qh, DK_D // 2, axis=1) * sin
            kr = (kh * cos + pltpu.roll(kh, DK_D // 2, axis=1) * sin) * (DK_D ** -0.5)
            vh = v_ref[r, slv]
            if mask is not None:
                vh = jnp.where(mask, vh, 0.0)
            vb = vh.astype(BF16)
            scores = jnp.where(causal, _dot_nt(qr.astype(BF16), kr.astype(BF16)) * jnp.exp(dist * lg), 0.0)
            state = s_ref[h]
            o = _dot(scores.astype(BF16), vb) + _dot((qr * jnp.exp(seen * lg)).astype(BF16), state.astype(BF16))
            k_end = (kr * jnp.exp((valid - seen) * lg)).astype(BF16)
            s_ref[h] = math.exp(valid * lg) * state + _dot_tn(k_end, vb)
            mu = jnp.mean(o, axis=-1, keepdims=True)
            oc = o - mu
            var = jnp.mean(oc * oc, axis=-1, keepdims=True)
            o_ref[r, slv] = (oc * lax.rsqrt(var + LN_EPS) * _silu(g_ref[r, slv])).astype(o_ref.dtype)
        return carry

    lax.fori_loop(0, n_chunks, body, 0)


def _rotary_tables(sq, start):
    half = DK_D // 2
    inv = ROPE_BASE ** (-jnp.arange(half, dtype=F32) / half)
    pos = start + jnp.arange(sq.t_pad, dtype=jnp.int32)
    ang = pos.astype(F32)[:, None] * inv
    cos, sin = jnp.cos(ang), jnp.sin(ang)
    cos2 = jnp.concatenate([cos, cos], axis=-1)
    sin2 = jnp.concatenate([-sin, sin], axis=-1)
    return cos2, sin2


def _retention(sq, u, s0, start):
    wq = H_D * DK_D
    wv = H_D * DV_D
    q_first = (2 * RG_WIDTH) // wq
    v_first = (2 * RG_WIDTH + 2 * wq) // wv

    def rows(width, first):
        return pl.BlockSpec((sq.tb, width), lambda b, t: (b * sq.nt + t, first))

    tab = pl.BlockSpec((sq.tb, DK_D), lambda b, t: (t, 0))
    state = pl.BlockSpec((None, H_D, DK_D, DV_D), lambda b, t: (b, 0, 0, 0))
    cos2, sin2 = _rotary_tables(sq, start)
    return pl.pallas_call(
        functools.partial(_ret_kernel, chunk=sq.chunk, valid=sq.valid, n_chunks=sq.n_chunks),
        grid=(sq.bsz, sq.nt),
        in_specs=[rows(wq, q_first), rows(wq, q_first + 1), rows(wv, v_first), rows(wv, v_first + 1), tab, tab, state],
        out_specs=[rows(wv, 0), state],
        out_shape=[jax.ShapeDtypeStruct((sq.m_pad, wv), BF16),
                   jax.ShapeDtypeStruct((sq.bsz, H_D, DK_D, DV_D), F32)],
        compiler_params=_cparams(2),
        name="retention",
    )(u, u, u, u, cos2, sin2, s0.astype(F32))


def _pad_rows(sq, a):
    if sq.t_pad == sq.t_len:
        return a
    a = a.reshape(sq.bsz, sq.t_len, a.shape[-1])
    return jnp.pad(a, ((0, 0), (0, sq.t_pad - sq.t_len), (0, 0))).reshape(sq.m_pad, a.shape[-1])


def _real_rows(sq, a):
    if sq.t_pad == sq.t_len:
        return a
    return a.reshape(sq.bsz, sq.t_pad, a.shape[-1])[:, :sq.t_len].reshape(sq.bsz * sq.t_len, a.shape[-1])


def _last_inputs(sq, u, first_col, width):
    assert sq.t_len >= CONV_W - 1
    u3 = u.reshape(sq.bsz, sq.t_pad, u.shape[-1])
    return u3[:, sq.t_len - (CONV_W - 1):sq.t_len, first_col:first_col + width]


def _mixer_ab(h, bsz, t_len, layer, s_hgrn, s_delta, s_dconv, p, wb):
    sq_a = _SeqLayout(bsz, t_len, CHUNK_A)
    sq_b = _SeqLayout(bsz, t_len, CHUNK_B)
    assert sq_a.t_pad == sq_b.t_pad
    hp = _pad_rows(sq_a, h)
    tm = min(sq_a.m_pad, 1024)
    u = _matmul(hp, wb['w_in0'], IN0_MAIN, F32, tm, 1024)
    gates = _matmul(hp, wb['w_in0_gates'], wb['w_in0_gates'].shape[1], F32, tm, wb['w_in0_gates'].shape[1])
    lb_all = jnp.cumsum(jax.nn.softmax(p['hgrn_lb_logits'].astype(F32), axis=0), axis=0)
    o_a, hgrn_new = _hgrn(sq_a, u, s_hgrn.astype(F32), lb_all[layer], p['hgrn_norm_w'].astype(F32))
    q, k, v = _delta_pre(sq_b, u, s_dconv, p['delta_conv_w'])
    o_b, delta_new = _delta(sq_b, q, k, v, u, gates, s_delta.astype(F32), p['delta_a_log'], p['delta_dt_bias'],
                            p['delta_norm_w'])
    dconv_new = _last_inputs(sq_b, u, 4 * H_A * DK_A, C_B)
    return _real_rows(sq_a, o_a), _real_rows(sq_b, o_b), (hgrn_new, delta_new, dconv_new)


def _mixer_cd(h, bsz, t_len, start, s_rg, s_rgconv, s_ret, p, wb):
    sq = _SeqLayout(bsz, t_len, CHUNK_D)
    hp = _pad_rows(sq, h)
    u = _matmul(hp, wb['w_in1'], IN1, F32, min(sq.m_pad, 1024), 1024)
    y_rg, rg_new = _rglru(sq, u, s_rgconv, s_rg, p, first_pos_is_zero=(start == 0))
    o_d, ret_new = _retention(sq, u, s_ret, start)
    rgconv_new = _last_inputs(sq, u, 0, RG_WIDTH)
    return _real_rows(sq, y_rg), _real_rows(sq, o_d), (rg_new, rgconv_new, ret_new)


def _trunk(x, mod_all, start, states, p, wb):
    s_hgrn, s_delta, s_dconv, s_rg, s_rgconv, s_ret = states
    bsz, t_len, _ = x.shape
    lay = _RowLayout(bsz, t_len)
    m = lay.m
    tm_mm = min(m, 1024)
    tm_ffn = min(m, 2048)
    tm_out = min(m, 512)
    x = x.reshape(m, D_MODEL)

    mods = [lay.mod_arg(mod_all[layer]) for layer in range(DEPTH)]
    h = _modulate(lay, x, mods[0], 0)
    for layer in range(DEPTH):
        lg, lb = p['ln_g'][layer], p['ln_b'][layer]
        mid = _ffn_in(h, wb['ffn_w_in'], layer, 0, tm_ffn)
        ffn = _matmul(mid, wb['ffn_w_out'], D_MODEL, F32, tm_out, 512, lead=(layer, 0))
        x, h = _post_norm(lay, x, ffn, mods[layer], 0, lg[0], lb[0], 0.5, (mods[layer], 1))
        if layer % 2 == 0:
            o_1, o_2, (s_hgrn, s_delta, s_dconv) = _mixer_ab(h, bsz, t_len, layer, s_hgrn, s_delta, s_dconv, p, wb)
            w_o = wb['w_out0']
        else:
            o_1, o_2, (s_rg, s_rgconv, s_ret) = _mixer_cd(h, bsz, t_len, start, s_rg, s_rgconv, s_ret, p, wb)
            w_o = wb['w_out1']
        mix = _matmul_halves(o_1, o_2, w_o, tm_mm, 1024)
        x, h = _post_norm(lay, x, mix, mods[layer], 1, lg[1], lb[1], 1.0, (mods[layer], 2))
        mid = _ffn_in(h, wb['ffn_w_in'], layer, 1, tm_ffn)
        ffn = _matmul(mid, wb['ffn_w_out'], D_MODEL, F32, tm_out, 512, lead=(layer, 1))
        nxt = (mods[layer + 1], 0) if layer + 1 < DEPTH else None
        x, h = _post_norm(lay, x, ffn, mods[layer], 2, lg[2], lb[2], 0.5, nxt)
    return x.reshape(bsz, t_len, D_MODEL), (s_hgrn, s_delta, s_dconv, s_rg, s_rgconv, s_ret)


def kernel(x_prompt, x_sample, c_prompt, c_sample, state_hgrn, state_delta, state_delta_conv, state_rglru, state_rglru_conv, state_ret, ada_w, ada_b, ln_g, ln_b, ffn_w_in, ffn_w_out, w_in0, w_out0, hgrn_lb_logits, hgrn_norm_w, delta_conv_w, delta_a_log, delta_dt_bias, delta_norm_w, w_in1, w_out1, rg_conv_w, rg_conv_b, rg_wa, rg_ba, rg_wx, rg_bx, rg_lambda):
    p = {
        'ln_g': ln_g, 'ln_b': ln_b, 'hgrn_lb_logits': hgrn_lb_logits, 'hgrn_norm_w': hgrn_norm_w,
        'delta_conv_w': delta_conv_w, 'delta_a_log': delta_a_log, 'delta_dt_bias': delta_dt_bias,
        'delta_norm_w': delta_norm_w, 'rg_conv_w': rg_conv_w, 'rg_conv_b': rg_conv_b, 'rg_wa': rg_wa,
        'rg_ba': rg_ba, 'rg_wx': rg_wx, 'rg_bx': rg_bx, 'rg_lambda': rg_lambda,
    }
    wb = {
        'ffn_w_in': ffn_w_in,
        'ffn_w_out': ffn_w_out.astype(BF16),
        'w_in0': w_in0.astype(BF16),
        'w_in0_gates': _delta_gate_weights(w_in0),
        'w_out0': w_out0.astype(BF16),
        'w_in1': w_in1.astype(BF16),
        'w_out1': w_out1.astype(BF16),
    }
    nb, ns = x_prompt.shape[0], x_sample.shape[0]
    c_all = jnp.concatenate([c_prompt, c_sample], axis=0).astype(F32)
    rows = nb + ns
    rows_pad = -(-rows // 16) * 16
    c_act = jnp.pad(jax.nn.silu(c_all), ((0, rows_pad - rows), (0, 0))).astype(BF16)
    mod = _ada(c_act, ada_w, ada_b)
    mod_prompt, mod_sample = mod[:, :nb], mod[:, nb:rows]

    prompt_states = (
        jnp.zeros((nb, H_A, DK_A, DV_A), F32),
        jnp.zeros((nb, H_B, DK_B, DV_B), F32),
        jnp.zeros((nb, CONV_W - 1, C_B), F32),
        jnp.zeros((nb, RG_WIDTH), F32),
        jnp.zeros((nb, CONV_W - 1, RG_WIDTH), F32),
        jnp.zeros((nb, H_D, DK_D, DV_D), F32),
    )
    y_prompt, ps = _trunk(x_prompt, mod_prompt, 0, prompt_states, p, wb)
    sample_states = (state_hgrn, state_delta, state_delta_conv, state_rglru, state_rglru_conv, state_ret)
    y_sample, ss = _trunk(x_sample, mod_sample, PAST_LEN, sample_states, p, wb)
    return (y_prompt, y_sample) + tuple(ps) + tuple(ss)
```

```python
import functools
import math

import jax
import jax.numpy as jnp
import numpy as np
from jax import lax
from jax.experimental import pallas as pl
from jax.experimental.pallas import tpu as pltpu

D_MODEL = 4096
DEPTH = 2
PAST_LEN = 16384
MIX_HALF = D_MODEL // 2
DK_A = 128
DV_A = 128
H_A = MIX_HALF // DV_A
DK_B = 128
DV_B = 128
H_B = MIX_HALF // DV_B
CONV_W = 4
C_B = 2 * H_B * DK_B + H_B * DV_B
RG_WIDTH = MIX_HALF
RG_BW = 128
RG_BLOCKS = RG_WIDTH // RG_BW
RG_C = 8.0
DK_D = 128
DV_D = 256
H_D = MIX_HALF // DV_D
D_FF = 11008
CHUNK_A = 16
CHUNK_B = 64
CHUNK_D = 64
ROPE_BASE = 10000.0
LN_EPS = 1e-5
RMS_EPS = 1e-6
ALPHA = (2 * DEPTH) ** 0.25

IN0_SIZES = (H_A * DK_A, H_A * DK_A, H_A * DV_A, H_A * DV_A, C_B, H_B * DV_B, H_B, H_B)
IN0 = sum(IN0_SIZES)
IN0_MAIN = IN0 - 2 * H_B
IN1_SIZES = (RG_WIDTH, RG_WIDTH, H_D * DK_D, H_D * DK_D, H_D * DV_D, H_D * DV_D)
IN1 = sum(IN1_SIZES)

F32 = jnp.float32
BF16 = jnp.bfloat16

V7X_VMEM_LIMIT_BYTES = 56 * 1024 * 1024
V7X_LANES = 128
SUBLANES = 8
FFN_TN = 256
HEADS_PER_STEP = 4
GATE_BETA_LANE = 16


def _cparams(n_axes):
    return pltpu.CompilerParams(dimension_semantics=("arbitrary",) * n_axes,
                                vmem_limit_bytes=V7X_VMEM_LIMIT_BYTES)


def _dot(a, b):
    return jnp.dot(a, b, preferred_element_type=F32)


def _dot_nt(a, b):
    return lax.dot_general(a, b, (((1,), (1,)), ((), ())), preferred_element_type=F32)


def _dot_tn(a, b):
    return lax.dot_general(a, b, (((0,), (0,)), ((), ())), preferred_element_type=F32)


def _split_bf16(x, terms):
    parts = []
    for _ in range(terms - 1):
        hi = x.astype(BF16)
        parts.append(hi)
        x = x - hi.astype(F32)
    parts.append(x.astype(BF16))
    return parts


def _dot_split(a, b):
    ah, al = _split_bf16(a, 2)
    bh, bl = _split_bf16(b, 2)
    return _dot(ah, bh) + (_dot(ah, bl) + _dot(al, bh))


def _dot_select(sel, x, nt=False):
    f = _dot_nt if nt else _dot
    x1, x2, x3 = _split_bf16(x, 3)
    return f(sel, x1) + (f(sel, x2) + f(sel, x3))


def _iota2(shape, axis):
    return lax.broadcasted_iota(jnp.int32, shape, axis)


def _mm_kernel(x_ref, w_ref, o_ref):
    o_ref[...] = jnp.dot(x_ref[...], w_ref[...], preferred_element_type=F32).astype(o_ref.dtype)


def _matmul(x, w, n_cols, out_dtype, tm, tn, lead=(), single_buffer_rows=False):
    m, k = x.shape
    assert m % tm == 0 and n_cols % tn == 0 and w.shape[len(lead)] == k
    row_mode = dict(pipeline_mode=pl.Buffered(1)) if single_buffer_rows else {}
    return pl.pallas_call(
        _mm_kernel,
        grid=(m // tm, n_cols // tn),
        in_specs=[pl.BlockSpec((tm, k), lambda i, j: (i, 0), **row_mode),
                  pl.BlockSpec((None,) * len(lead) + (k, tn), lambda i, j: lead + (0, j))],
        out_specs=pl.BlockSpec((tm, tn), lambda i, j: (i, j)),
        out_shape=jax.ShapeDtypeStruct((m, n_cols), out_dtype),
        compiler_params=_cparams(2),
        name="matmul",
    )(x, w)


def _mm2_kernel(xa_ref, xb_ref, wa_ref, wb_ref, o_ref):
    o_ref[...] = (jnp.dot(xa_ref[...], wa_ref[...], preferred_element_type=F32)
                  + jnp.dot(xb_ref[...], wb_ref[...], preferred_element_type=F32)).astype(o_ref.dtype)


def _matmul_halves(xa, xb, w, tm, tn):
    m, kh = xa.shape
    n = w.shape[1]
    return pl.pallas_call(
        _mm2_kernel,
        grid=(m // tm, n // tn),
        in_specs=[pl.BlockSpec((tm, kh), lambda i, j: (i, 0)), pl.BlockSpec((tm, kh), lambda i, j: (i, 0)),
                  pl.BlockSpec((kh, tn), lambda i, j: (0, j)), pl.BlockSpec((kh, tn), lambda i, j: (1, j))],
        out_specs=pl.BlockSpec((tm, tn), lambda i, j: (i, j)),
        out_shape=jax.ShapeDtypeStruct((m, n), F32),
        compiler_params=_cparams(2),
        name="matmul_halves",
    )(xa, xb, w, w)


def _swiglu_kernel(x_ref, wg_ref, wu_ref, o_ref):
    x = x_ref[...]
    gate = jnp.dot(x, wg_ref[...].astype(BF16), preferred_element_type=F32)
    up = jnp.dot(x, wu_ref[...].astype(BF16), preferred_element_type=F32)
    o_ref[...] = (gate * jax.nn.sigmoid(gate) * up).astype(o_ref.dtype)


def _ffn_in(x, ffn_w_in, layer, half, tm):
    m, k = x.shape
    n_tiles = D_FF // FFN_TN
    return pl.pallas_call(
        _swiglu_kernel,
        grid=(m // tm, n_tiles),
        in_specs=[pl.BlockSpec((tm, k), lambda i, j: (i, 0), pipeline_mode=pl.Buffered(1)),
                  pl.BlockSpec((None, None, k, FFN_TN), lambda i, j: (layer, half, 0, j)),
                  pl.BlockSpec((None, None, k, FFN_TN), lambda i, j: (layer, half, 0, j + n_tiles))],
        out_specs=pl.BlockSpec((tm, FFN_TN), lambda i, j: (i, j)),
        out_shape=jax.ShapeDtypeStruct((m, D_FF), BF16),
        compiler_params=_cparams(2),
        name="ffn_in",
    )(x, ffn_w_in, ffn_w_in)


def _ada_kernel(c_ref, w_ref, b_ref, o_ref):
    o_ref[...] = jnp.dot(c_ref[...], w_ref[...].astype(BF16), preferred_element_type=F32) + b_ref[...]


def _ada(c_act, ada_w, ada_b, tn=512):
    r = c_act.shape[0]
    n = ada_w.shape[2]
    return pl.pallas_call(
        _ada_kernel,
        grid=(DEPTH, n // tn),
        in_specs=[pl.BlockSpec((r, D_MODEL), lambda l, j: (0, 0)),
                  pl.BlockSpec((None, D_MODEL, tn), lambda l, j: (l, 0, j)),
                  pl.BlockSpec((None, 1, tn), lambda l, j: (l, 0, j))],
        out_specs=pl.BlockSpec((None, r, tn), lambda l, j: (l, 0, j)),
        out_shape=jax.ShapeDtypeStruct((DEPTH, r, n), F32),
        compiler_params=_cparams(2),
        name="ada",
    )(c_act, ada_w, ada_b.reshape(DEPTH, 1, n))


def _per_row(mod_ref, rows, rep):
    v = mod_ref[...]
    if rep == 1:
        return v
    sel = (_iota2((rows, rows // rep), 0) // rep == _iota2((rows, rows // rep), 1)).astype(BF16)
    return _dot_select(sel, v)


def _modulate_kernel(x_ref, shift_ref, scale_ref, h_ref, *, rep):
    rows = x_ref.shape[0]
    h_ref[...] = (x_ref[...] * (1.0 + _per_row(scale_ref, rows, rep)) + _per_row(shift_ref, rows, rep)).astype(h_ref.dtype)


def _post_norm_body(x_ref, out_ref, gate_ref, g_ref, b_ref, rho, rep):
    z = ALPHA * x_ref[...] + rho * _per_row(gate_ref, x_ref.shape[0], rep) * out_ref[...]
    mu = jnp.mean(z, axis=-1, keepdims=True)
    zc = z - mu
    var = jnp.mean(zc * zc, axis=-1, keepdims=True)
    return zc * lax.rsqrt(var + LN_EPS) * g_ref[...] + b_ref[...]


def _post_norm_mod_kernel(x_ref, out_ref, gate_ref, g_ref, b_ref, shift_ref, scale_ref, xn_ref, h_ref, *, rho, rep):
    rows = x_ref.shape[0]
    xn = _post_norm_body(x_ref, out_ref, gate_ref, g_ref, b_ref, rho, rep)
    xn_ref[...] = xn
    h_ref[...] = (xn * (1.0 + _per_row(scale_ref, rows, rep)) + _per_row(shift_ref, rows, rep)).astype(h_ref.dtype)


def _post_norm_kernel(x_ref, out_ref, gate_ref, g_ref, b_ref, xn_ref, *, rho, rep):
    xn_ref[...] = _post_norm_body(x_ref, out_ref, gate_ref, g_ref, b_ref, rho, rep)


class _RowLayout:
    def __init__(self, bsz, t_len):
        self.bsz, self.t_len = bsz, t_len
        self.m = bsz * t_len
        self.per_tile = t_len >= 256
        self.tm = min(self.m, 256)
        self.rep = 1 if self.per_tile else t_len
        assert self.per_tile or (self.tm % t_len == 0 and (self.tm // t_len) % SUBLANES == 0)

    def mod_arg(self, mod_layer):
        if self.per_tile:
            return mod_layer.reshape(self.bsz, 1, 9 * D_MODEL)
        return mod_layer

    def mod_spec(self, col):
        if self.per_tile:
            per = self.t_len // self.tm
            return pl.BlockSpec((None, 1, D_MODEL), lambda i: (i // per, 0, col))
        return pl.BlockSpec((self.tm // self.t_len, D_MODEL), lambda i: (i, col))

    def row_spec(self):
        return pl.BlockSpec((self.tm, D_MODEL), lambda i: (i, 0))

    @staticmethod
    def vec_spec():
        return pl.BlockSpec((1, D_MODEL), lambda i: (0, 0))


def _modulate(lay, x, mods, sub):
    return pl.pallas_call(
        functools.partial(_modulate_kernel, rep=lay.rep),
        grid=(lay.m // lay.tm,),
        in_specs=[lay.row_spec(), lay.mod_spec(3 * sub), lay.mod_spec(3 * sub + 1)],
        out_specs=lay.row_spec(),
        out_shape=jax.ShapeDtypeStruct((lay.m, D_MODEL), BF16),
        compiler_params=_cparams(1),
        name="modulate",
    )(x, mods, mods)


def _post_norm(lay, x, out, mods, sub, g, b, rho, nxt):
    g2, b2 = g.reshape(1, D_MODEL), b.reshape(1, D_MODEL)
    common = dict(grid=(lay.m // lay.tm,), compiler_params=_cparams(1))
    base_specs = [lay.row_spec(), lay.row_spec(), lay.mod_spec(3 * sub + 2), lay.vec_spec(), lay.vec_spec()]
    if nxt is None:
        return pl.pallas_call(
            functools.partial(_post_norm_kernel, rho=rho, rep=lay.rep),
            in_specs=base_specs,
            out_specs=lay.row_spec(),
            out_shape=jax.ShapeDtypeStruct((lay.m, D_MODEL), F32),
            name="post_norm", **common,
        )(x, out, mods, g2, b2), None
    nxt_mods, nxt_sub = nxt
    return pl.pallas_call(
        functools.partial(_post_norm_mod_kernel, rho=rho, rep=lay.rep),
        in_specs=base_specs + [lay.mod_spec(3 * nxt_sub), lay.mod_spec(3 * nxt_sub + 1)],
        out_specs=[lay.row_spec(), lay.row_spec()],
        out_shape=[jax.ShapeDtypeStruct((lay.m, D_MODEL), F32), jax.ShapeDtypeStruct((lay.m, D_MODEL), BF16)],
        name="post_norm_mod", **common,
    )(x, out, mods, g2, b2, nxt_mods, nxt_mods)


def _sigmoid(x):
    return 1.0 / (1.0 + jnp.exp(-x))


def _silu(x):
    return x * _sigmoid(x)


def _softplus(x):
    return jnp.maximum(x, 0.0) + jnp.log1p(jnp.exp(-jnp.abs(x)))


def _neg_expm1(x):
    return -jnp.tanh(0.5 * x) * (jnp.exp(x) + 1.0)


class _SeqLayout:
    def __init__(self, bsz, t_len, chunk_max):
        self.bsz, self.t_len = bsz, t_len
        blk = math.gcd(t_len, chunk_max)
        if blk % SUBLANES == 0:
            self.t_pad, self.chunk, self.valid = t_len, blk, blk
        else:
            assert blk == t_len and t_len < SUBLANES
            self.t_pad, self.chunk, self.valid = SUBLANES, SUBLANES, t_len
        self.tb = min(self.t_pad, 256)
        assert self.t_pad % self.tb == 0 and self.tb % self.chunk == 0
        self.nt = self.t_pad // self.tb
        self.n_chunks = self.tb // self.chunk
        self.m_pad = bsz * self.t_pad


def _row_mask(chunk, valid):
    return None if valid == chunk else (_iota2((chunk, 1), 0) < valid)


def _hgrn_kernel(q_ref, f_ref, i_ref, g_ref, s0_ref, lb_ref, nw_ref, o_ref, s_ref,
                 oi_ref, qd_ref, ke_ref, vb_ref, ge_ref, st_ref, *, tb, chunk, valid, chunks_per_req, carried):
    hg = HEADS_PER_STEP
    n_chunks = tb // chunk
    shift = int(math.log2(chunk))
    assert 1 << shift == chunk

    if carried:
        @pl.when(pl.program_id(2) == 0)
        def _():
            for h in range(hg):
                st_ref[h] = s0_ref[0, h].T
    else:
        s_ref[...] = s0_ref[...]

    lb = lb_ref[...]
    nw = nw_ref[...]
    row = _iota2((tb, tb), 0)
    col = _iota2((tb, tb), 1)
    same = (row >> shift) == (col >> shift)
    incl = jnp.logical_and(same, row >= col)

    f = lb + (1.0 - lb) * _sigmoid(f_ref[...])
    log_f = jnp.log(f)
    k = 1.0 - f
    v = i_ref[...]
    if valid != chunk:
        real = (_iota2((tb, 1), 0) & (chunk - 1)) < valid
        log_f = jnp.where(real, log_f, 0.0)
        v = jnp.where(real, v, 0.0)
    cum = _dot_select(incl.astype(BF16), log_f)
    cum_tot = _dot_select(same.astype(BF16), log_f)
    g_end = jnp.exp(cum_tot)
    ge_ref[...] = g_end
    q_dec = (_silu(q_ref[...]) * jnp.exp(cum)).astype(BF16)
    k_inv = (k * jnp.exp(-cum)).astype(BF16)
    k_end = k * jnp.exp(cum_tot - cum)
    vb = v.astype(BF16)
    eye_dk = (_iota2((DK_A, DK_A), 0) == _iota2((DK_A, DK_A), 1)).astype(BF16)
    g_cols = []
    for h in range(hg):
        sl = slice(h * DK_A, (h + 1) * DK_A)
        scores = jnp.where(incl, _dot_nt(q_dec[:, sl], k_inv[:, sl]), 0.0)
        oi_ref[h] = _dot(scores.astype(BF16), vb[:, sl])
        qd_ref[h] = q_dec[:, sl].astype(qd_ref.dtype)
        ke_ref[h] = k_end[:, sl].astype(ke_ref.dtype)
        vb_ref[h] = v[:, sl].astype(vb_ref.dtype)
        if not carried:
            g_cols.append(_dot_select(eye_dk, g_end[:, sl], nt=True))

    def finish(h, r, o):
        sl = slice(h * DK_A, (h + 1) * DK_A)
        ms = jnp.mean(o * o, axis=-1, keepdims=True)
        o_ref[r, sl] = (o * lax.rsqrt(ms + RMS_EPS) * nw[:, sl] * _silu(g_ref[r, sl])).astype(o_ref.dtype)

    if carried:
        def body(j, carry):
            r = pl.ds(pl.multiple_of(j * chunk, chunk), chunk)
            g_rows = ge_ref[r, :]
            for h in range(hg):
                sl = slice(h * DK_A, (h + 1) * DK_A)
                st = st_ref[h]
                o = oi_ref[h, r, :] + _dot_nt(qd_ref[h, r, :].astype(BF16), st.astype(BF16))
                st_ref[h] = st * g_rows[0:1, sl] + _dot_tn(vb_ref[h, r, :].astype(BF16), ke_ref[h, r, :].astype(BF16))
                finish(h, r, o)
            return carry

        lax.fori_loop(0, n_chunks, body, 0, unroll=4)

        @pl.when(pl.program_id(2) == pl.num_programs(2) - 1)
        def _():
            for h in range(hg):
                s_ref[0, h] = st_ref[h].T
    else:
        for j in range(n_chunks):
            r = slice(j * chunk, (j + 1) * chunk)
            req = j // chunks_per_req
            for h in range(hg):
                state = s_ref[req, h]
                o = oi_ref[h, r, :] + _dot(qd_ref[h, r, :].astype(BF16), state.astype(BF16))
                s_ref[req, h] = (g_cols[h][:, j * chunk:j * chunk + 1] * state
                                 + _dot_tn(ke_ref[h, r, :].astype(BF16), vb_ref[h, r, :].astype(BF16)))
                finish(h, r, o)


def _hgrn(sq, u, s0, lb, norm_w):
    hg = HEADS_PER_STEP
    wl = hg * DK_A
    nsec = (H_A * DK_A) // wl
    if sq.valid == sq.chunk:
        tb, reqs, nb, nt, carried = sq.tb, 1, sq.bsz, sq.nt, True
    else:
        assert sq.t_pad == sq.chunk
        reqs = math.gcd(sq.bsz, 128 // sq.t_pad)
        tb, nb, nt, carried = reqs * sq.t_pad, sq.bsz // reqs, 1, False
    chunks_per_req = (tb // sq.chunk) // reqs
    row_dt = BF16 if sq.chunk % (2 * SUBLANES) == 0 else F32

    def sec(s):
        return pl.BlockSpec((tb, wl), lambda b, g, t: (b * nt + t, s * nsec + g))

    state_spec = pl.BlockSpec((reqs, hg, DK_A, DV_A), lambda b, g, t: (b, g, 0, 0))
    vec_spec = pl.BlockSpec((1, wl), lambda b, g, t: (0, g))
    return pl.pallas_call(
        functools.partial(_hgrn_kernel, tb=tb, chunk=sq.chunk, valid=sq.valid, chunks_per_req=chunks_per_req,
                          carried=carried),
        grid=(nb, nsec, nt),
        in_specs=[sec(0), sec(1), sec(2), sec(3), state_spec, vec_spec, vec_spec],
        out_specs=[pl.BlockSpec((tb, wl), lambda b, g, t: (b * nt + t, g)), state_spec],
        out_shape=[jax.ShapeDtypeStruct((sq.m_pad, H_A * DV_A), BF16),
                   jax.ShapeDtypeStruct((sq.bsz, H_A, DK_A, DV_A), F32)],
        scratch_shapes=[pltpu.VMEM((hg, tb, DV_A), F32), pltpu.VMEM((hg, tb, DK_A), row_dt),
                        pltpu.VMEM((hg, tb, DK_A), row_dt), pltpu.VMEM((hg, tb, DV_A), row_dt),
                        pltpu.VMEM((tb, wl), F32), pltpu.VMEM((hg, DV_A, DK_A), F32)],
        compiler_params=_cparams(3),
        name="hgrn",
    )(u, u, u, u, s0, lb.reshape(1, -1), norm_w.reshape(1, -1))


def _delta_pre_kernel(qr_ref, kr_ref, vr_ref, cs_ref, w_ref, q_ref, k_ref, v_ref, carry_ref, *, tb):
    t = pl.program_id(1)
    width = H_B * DK_B

    @pl.when(t == 0)
    def _():
        for s in range(3):
            carry_ref[s] = cs_ref[:, s * width:(s + 1) * width]

    for s, (x_ref, y_ref) in enumerate(((qr_ref, q_ref), (kr_ref, k_ref), (vr_ref, v_ref))):
        w = w_ref[:, s * width:(s + 1) * width]
        x = x_ref[...]
        xx = jnp.concatenate([carry_ref[s], x], axis=0)
        y = w[CONV_W - 1:CONV_W] * x
        for j in range(1, CONV_W):
            y = y + w[CONV_W - 1 - j:CONV_W - j] * xx[SUBLANES - j:SUBLANES - j + tb]
        carry_ref[s] = x[tb - SUBLANES:tb]
        y = _silu(y)
        if s == 2:
            y_ref[...] = y
        else:
            scale = DK_B ** -0.5 if s == 0 else 1.0
            for h in range(H_B):
                sl = slice(h * DK_B, (h + 1) * DK_B)
                yh = y[:, sl]
                ss = jnp.sum(yh * yh, axis=-1, keepdims=True)
                y_ref[:, sl] = yh * (lax.rsqrt(ss + RMS_EPS) * scale)


def _delta_pre(sq, u, conv_state, conv_w):
    width = H_B * DK_B
    first = (4 * H_A * DK_A) // width

    def sec(s):
        return pl.BlockSpec((sq.tb, width), lambda b, t: (b * sq.nt + t, first + s))

    out_spec = pl.BlockSpec((sq.tb, width), lambda b, t: (b * sq.nt + t, 0))
    cs = jnp.pad(conv_state.astype(F32), ((0, 0), (SUBLANES - (CONV_W - 1), 0), (0, 0)))
    return pl.pallas_call(
        functools.partial(_delta_pre_kernel, tb=sq.tb),
        grid=(sq.bsz, sq.nt),
        in_specs=[sec(0), sec(1), sec(2),
                  pl.BlockSpec((None, SUBLANES, C_B), lambda b, t: (b, 0, 0)),
                  pl.BlockSpec((CONV_W, C_B), lambda b, t: (0, 0))],
        out_specs=[out_spec, out_spec, out_spec],
        out_shape=[jax.ShapeDtypeStruct((sq.m_pad, width), F32)] * 3,
        scratch_shapes=[pltpu.VMEM((3, SUBLANES, width), F32)],
        compiler_params=_cparams(2),
        name="delta_pre",
    )(u, u, u, cs, conv_w.astype(F32))


def _delta_kernel(q_ref, k_ref, v_ref, z_ref, gt_ref, s0_ref, alog_ref, dtb_ref, nw_ref, o_ref, s_ref,
                  wv_ref, wk_ref, qd_ref, ke_ref, qk_ref, u_ref, ge_ref, *, tb, chunk, valid, chunks_per_req, carried):
    hg = HEADS_PER_STEP
    n_chunks = tb // chunk
    shift = int(math.log2(chunk))
    assert 1 << shift == chunk

    if carried:
        @pl.when(pl.program_id(2) == 0)
        def _():
            s_ref[...] = s0_ref[...]
    else:
        s_ref[...] = s0_ref[...]

    neg_a = -jnp.exp(alog_ref[...])
    dtb = dtb_ref[...]
    nw = nw_ref[...]
    row = _iota2((tb, tb), 0)
    col = _iota2((tb, tb), 1)
    same = (row >> shift) == (col >> shift)
    incl = jnp.logical_and(same, row >= col)
    strict = jnp.logical_and(same, row > col)
    tri_bd = incl.astype(BF16)
    ones_bd = same.astype(BF16)
    eye = (row == col).astype(F32)
    pick = (_iota2((SUBLANES, V7X_LANES), 0) == _iota2((SUBLANES, V7X_LANES), 1)).astype(BF16)
    n_sq = max(shift - 1, 0)

    gt = gt_ref[...]
    log_g = neg_a * _softplus(gt + dtb)
    beta_all = _sigmoid(gt)
    if valid != chunk:
        real = (_iota2((tb, 1), 0) & (chunk - 1)) < valid
        log_g = jnp.where(real, log_g, 0.0)
        beta_all = jnp.where(real, beta_all, 0.0)
    cum = _dot_select(tri_bd, log_g)
    cum_tot = _dot_select(ones_bd, log_g)
    cum_t = _dot_select(pick, cum, nt=True)
    ge_ref[...] = jnp.exp(cum_tot)
    k_end_scale = jnp.exp(cum_tot - cum)
    e_all = jnp.exp(cum)

    for h in range(hg):
        sl = slice(h * DK_B, (h + 1) * DK_B)
        c_col = cum[:, h:h + 1]
        c_row = cum_t[h:h + 1, :]
        beta = beta_all[:, GATE_BETA_LANE + h:GATE_BETA_LANE + h + 1]
        decay = jnp.where(incl, jnp.exp(jnp.where(incl, c_col - c_row, 0.0)), 0.0)
        qh, kh, vh = q_ref[:, sl], k_ref[:, sl], v_ref[:, sl]
        qb, kb = qh.astype(BF16), kh.astype(BF16)
        nmat = jnp.where(strict, beta * _dot_nt(kb, kb) * decay, 0.0)
        inv = eye - nmat
        power = nmat
        for _ in range(n_sq):
            power = _dot_split(power, power)
            inv = inv + _dot_split(inv, power)
        e_col = e_all[:, h:h + 1]
        rhs = jnp.concatenate([beta * vh, (beta * e_col) * kh], axis=1)
        w = _dot_split(inv, rhs)
        wv_ref[h] = w[:, :DV_B]
        wk_ref[h] = w[:, DV_B:].astype(wk_ref.dtype)
        qd_ref[h] = (qh * e_col).astype(qd_ref.dtype)
        ke_ref[h] = (kh * k_end_scale[:, h:h + 1]).astype(ke_ref.dtype)
        qk_ref[h] = (_dot_nt(qb, kb) * decay).astype(qk_ref.dtype)
        u_ref[h] = jnp.zeros((tb, DV_B), u_ref.dtype)

    def walk(r, req):
        g_rows = ge_ref[r, :]
        for h in range(hg):
            sl = slice(h * DK_B, (h + 1) * DK_B)
            state = s_ref[req, h]
            sb = state.astype(BF16)
            both = _dot(jnp.concatenate([wk_ref[h, r, :], qd_ref[h, r, :]], axis=0).astype(BF16), sb)
            u = wv_ref[h, r, :] - both[:chunk]
            ub = u.astype(BF16)
            u_ref[h, r, :] = u.astype(u_ref.dtype)
            o = both[chunk:] + _dot(qk_ref[h, r, :].astype(BF16), u_ref[h].astype(BF16))
            s_ref[req, h] = g_rows[0:1, h:h + 1] * state + _dot_tn(ke_ref[h, r, :].astype(BF16), ub)
            ms = jnp.mean(o * o, axis=-1, keepdims=True)
            o_ref[r, sl] = (o * lax.rsqrt(ms + RMS_EPS) * nw * _silu(z_ref[r, sl])).astype(o_ref.dtype)

    if carried:
        def body(j, carry):
            walk(pl.ds(pl.multiple_of(j * chunk, chunk), chunk), 0)
            return carry

        lax.fori_loop(0, n_chunks, body, 0)
    else:
        for j in range(n_chunks):
            walk(slice(j * chunk, (j + 1) * chunk), j // chunks_per_req)


def _delta_gate_lanes(v, fill):
    hg = HEADS_PER_STEP
    v = v.astype(F32).reshape(H_B // hg, 1, hg)
    return jnp.pad(v, ((0, 0), (0, 0), (0, V7X_LANES - hg)), constant_values=fill)


def _delta_gate_weights(w_in0):
    hg = HEADS_PER_STEP
    ab = w_in0[:, IN0_MAIN:IN0_MAIN + H_B]
    bb = w_in0[:, IN0_MAIN + H_B:]
    tiles = []
    for g in range(H_B // hg):
        tile = jnp.zeros((D_MODEL, V7X_LANES), w_in0.dtype)
        tile = tile.at[:, :hg].set(ab[:, g * hg:(g + 1) * hg])
        tile = tile.at[:, GATE_BETA_LANE:GATE_BETA_LANE + hg].set(bb[:, g * hg:(g + 1) * hg])
        tiles.append(tile)
    return jnp.concatenate(tiles, axis=1).astype(BF16)


def _delta(sq, q, k, v, u, gates, s0, a_log, dt_bias, norm_w):
    hg = HEADS_PER_STEP
    wl = hg * DK_B
    ngrp = H_B // hg
    z_first = (4 * H_A * DK_A + C_B) // wl
    if sq.valid == sq.chunk:
        tb, reqs, nb, nt, carried = sq.tb, 1, sq.bsz, sq.nt, True
    else:
        assert sq.t_pad == sq.chunk
        reqs = math.gcd(sq.bsz, 128 // sq.t_pad)
        tb, nb, nt, carried = reqs * sq.t_pad, sq.bsz // reqs, 1, False
    chunks_per_req = (tb // sq.chunk) // reqs
    row_dt = BF16 if sq.chunk % (2 * SUBLANES) == 0 else F32

    def rows(first):
        return pl.BlockSpec((tb, wl), lambda b, g, t: (b * nt + t, first + g))

    state_spec = pl.BlockSpec((reqs, hg, DK_B, DV_B), lambda b, g, t: (b, g, 0, 0))
    lane_spec = pl.BlockSpec((None, 1, V7X_LANES), lambda b, g, t: (g, 0, 0))
    return pl.pallas_call(
        functools.partial(_delta_kernel, tb=tb, chunk=sq.chunk, valid=sq.valid, chunks_per_req=chunks_per_req,
                          carried=carried),
        grid=(nb, ngrp, nt),
        in_specs=[rows(0), rows(0), rows(0), rows(z_first),
                  pl.BlockSpec((tb, V7X_LANES), lambda b, g, t: (b * nt + t, g)),
                  state_spec, lane_spec, lane_spec,
                  pl.BlockSpec((1, DV_B), lambda b, g, t: (0, 0))],
        out_specs=[rows(0), state_spec],
        out_shape=[jax.ShapeDtypeStruct((sq.m_pad, H_B * DV_B), BF16),
                   jax.ShapeDtypeStruct((sq.bsz, H_B, DK_B, DV_B), F32)],
        scratch_shapes=[pltpu.VMEM((hg, tb, DV_B), F32), pltpu.VMEM((hg, tb, DK_B), row_dt),
                        pltpu.VMEM((hg, tb, DK_B), row_dt), pltpu.VMEM((hg, tb, DK_B), row_dt),
                        pltpu.VMEM((hg, tb, tb), row_dt), pltpu.VMEM((hg, tb, DV_B), row_dt),
                        pltpu.VMEM((tb, V7X_LANES), F32)],
        compiler_params=_cparams(3),
        name="delta",
    )(q, k, v, u, gates, s0, _delta_gate_lanes(a_log, 0.0), _delta_gate_lanes(dt_bias, 0.0),
      norm_w.astype(F32).reshape(1, DV_B))


def _rglru_kernel(x_ref, gate_ref, cs_ref, h0_ref, cw_ref, cb_ref, wa_ref, wx_ref, ba_ref, bx_ref, lam_ref,
                  y_ref, hl_ref, carry_ref, h_ref, a_ref, b_ref, *, tb, valid_rows, first_pos_is_zero):
    t = pl.program_id(1)

    @pl.when(t == 0)
    def _():
        carry_ref[...] = cs_ref[...]
        h_ref[...] = h0_ref[...]

    w = cw_ref[...]
    x = x_ref[...]
    xx = jnp.concatenate([carry_ref[...], x], axis=0)
    xr = w[CONV_W - 1:CONV_W] * x + cb_ref[...]
    for j in range(1, CONV_W):
        xr = xr + w[CONV_W - 1 - j:CONV_W - j] * xx[SUBLANES - j:SUBLANES - j + tb]
    carry_ref[...] = x[tb - SUBLANES:tb]
    neg_c_sp = -RG_C * _softplus(-lam_ref[...])
    for n in range(RG_BLOCKS):
        sl = slice(n * RG_BW, (n + 1) * RG_BW)
        xn = xr[:, sl]
        xb = xn.astype(BF16)
        r = _sigmoid(_dot(xb, wa_ref[n]) + ba_ref[:, sl])
        i = _sigmoid(_dot(xb, wx_ref[n]) + bx_ref[:, sl])
        log_a = neg_c_sp[:, sl] * r
        mult = jnp.sqrt(_neg_expm1(2.0 * log_a))
        if first_pos_is_zero:
            is_first = jnp.logical_and(t == 0, _iota2((tb, 1), 0) == 0)
            mult = jnp.where(is_first, 1.0, mult)
        a_ref[:, sl] = jnp.exp(log_a)
        b_ref[:, sl] = mult * i * xn

    def step(s, h):
        h = a_ref[pl.ds(s, 1), :] * h + b_ref[pl.ds(s, 1), :]
        b_ref[pl.ds(s, 1), :] = h
        return h

    h_ref[...] = lax.fori_loop(0, valid_rows, step, h_ref[...])
    y_ref[...] = (b_ref[...] * jax.nn.gelu(gate_ref[...], approximate=True)).astype(y_ref.dtype)

    @pl.when(t == pl.num_programs(1) - 1)
    def _():
        hl_ref[...] = h_ref[...]


def _rglru(sq, u, conv_state, h0, p, first_pos_is_zero):
    def sec(s):
        return pl.BlockSpec((sq.tb, RG_WIDTH), lambda b, t: (b * sq.nt + t, s))

    vec = pl.BlockSpec((1, RG_WIDTH), lambda b, t: (0, 0))
    blk = pl.BlockSpec((RG_BLOCKS, RG_BW, RG_BW), lambda b, t: (0, 0, 0))
    state = pl.BlockSpec((None, 1, RG_WIDTH), lambda b, t: (b, 0, 0))
    cs = jnp.pad(conv_state.astype(F32), ((0, 0), (SUBLANES - (CONV_W - 1), 0), (0, 0)))
    valid_rows = sq.tb if sq.valid == sq.chunk else sq.valid
    y, h_last = pl.pallas_call(
        functools.partial(_rglru_kernel, tb=sq.tb, valid_rows=valid_rows, first_pos_is_zero=first_pos_is_zero),
        grid=(sq.bsz, sq.nt),
        in_specs=[sec(0), sec(1),
                  pl.BlockSpec((None, SUBLANES, RG_WIDTH), lambda b, t: (b, 0, 0)), state,
                  pl.BlockSpec((CONV_W, RG_WIDTH), lambda b, t: (0, 0)), vec, blk, blk, vec, vec, vec],
        out_specs=[pl.BlockSpec((sq.tb, RG_WIDTH), lambda b, t: (b * sq.nt + t, 0)), state],
        out_shape=[jax.ShapeDtypeStruct((sq.m_pad, RG_WIDTH), BF16),
                   jax.ShapeDtypeStruct((sq.bsz, 1, RG_WIDTH), F32)],
        scratch_shapes=[pltpu.VMEM((SUBLANES, RG_WIDTH), F32), pltpu.VMEM((1, RG_WIDTH), F32),
                        pltpu.VMEM((sq.tb, RG_WIDTH), F32), pltpu.VMEM((sq.tb, RG_WIDTH), F32)],
        compiler_params=_cparams(2),
        name="rglru",
    )(u, u, cs, h0.astype(F32).reshape(sq.bsz, 1, RG_WIDTH), p['rg_conv_w'].astype(F32),
      p['rg_conv_b'].astype(F32).reshape(1, -1), p['rg_wa'].astype(BF16), p['rg_wx'].astype(BF16),
      p['rg_ba'].astype(F32).reshape(1, -1), p['rg_bx'].astype(F32).reshape(1, -1),
      p['rg_lambda'].astype(F32).reshape(1, -1))
    return y, h_last.reshape(sq.bsz, RG_WIDTH)


def _ret_kernel(q_ref, k_ref, v_ref, g_ref, cos_ref, sin_ref, s0_ref, o_ref, s_ref, *, chunk, valid, n_chunks):
    t = pl.program_id(1)

    @pl.when(t == 0)
    def _():
        s_ref[...] = s0_ref[...]

    row = _iota2((chunk, chunk), 0)
    col = _iota2((chunk, chunk), 1)
    causal = row >= col
    seen = jnp.minimum(_iota2((chunk, 1), 0) + 1, valid).astype(F32)
    dist = (row - col).astype(F32)
    mask = _row_mask(chunk, valid)
    log_gamma = [math.log1p(-2.0 ** (-5.0 - h)) for h in range(H_D)]

    def body(c, carry):
        r = pl.ds(pl.multiple_of(c * chunk, chunk), chunk)
        cos = cos_ref[r, :]
        sin = sin_ref[r, :]
        for h in range(H_D):
            lg = log_gamma[h]
            slk = slice(h * DK_D, (h + 1) * DK_D)
            slv = slice(h * DV_D, (h + 1) * DV_D)
            qh, kh = q_ref[r, slk], k_ref[r, slk]
            qr = qh * cos + pltpu.roll(qh, DK_D // 2, axis=1) * sin
            kr = (kh * cos + pltpu.roll(kh, DK_D // 2, axis=1) * sin) * (DK_D ** -0.5)
            vh = v_ref[r, slv]
            if mask is not None:
                vh = jnp.where(mask, vh, 0.0)
            vb = vh.astype(BF16)
            scores = jnp.where(causal, _dot_nt(qr.astype(BF16), kr.astype(BF16)) * jnp.exp(dist * lg), 0.0)
            state = s_ref[h]
            o = _dot(scores.astype(BF16), vb) + _dot((qr * jnp.exp(seen * lg)).astype(BF16), state.astype(BF16))
            k_end = (kr * jnp.exp((valid - seen) * lg)).astype(BF16)
            s_ref[h] = math.exp(valid * lg) * state + _dot_tn(k_end, vb)
            mu = jnp.mean(o, axis=-1, keepdims=True)
            oc = o - mu
            var = jnp.mean(oc * oc, axis=-1, keepdims=True)
            o_ref[r, slv] = (oc * lax.rsqrt(var + LN_EPS) * _silu(g_ref[r, slv])).astype(o_ref.dtype)
        return carry

    lax.fori_loop(0, n_chunks, body, 0)


def _rotary_tables(sq, start):
    half = DK_D // 2
    inv = ROPE_BASE ** (-jnp.arange(half, dtype=F32) / half)
    pos = start + jnp.arange(sq.t_pad, dtype=jnp.int32)
    ang = pos.astype(F32)[:, None] * inv
    cos, sin = jnp.cos(ang), jnp.sin(ang)
    cos2 = jnp.concatenate([cos, cos], axis=-1)
    sin2 = jnp.concatenate([-sin, sin], axis=-1)
    return cos2, sin2


def _retention(sq, u, s0, start):
    wq = H_D * DK_D
    wv = H_D * DV_D
    q_first = (2 * RG_WIDTH) // wq
    v_first = (2 * RG_WIDTH + 2 * wq) // wv

    def rows(width, first):
        return pl.BlockSpec((sq.tb, width), lambda b, t: (b * sq.nt + t, first))

    tab = pl.BlockSpec((sq.tb, DK_D), lambda b, t: (t, 0))
    state = pl.BlockSpec((None, H_D, DK_D, DV_D), lambda b, t: (b, 0, 0, 0))
    cos2, sin2 = _rotary_tables(sq, start)
    return pl.pallas_call(
        functools.partial(_ret_kernel, chunk=sq.chunk, valid=sq.valid, n_chunks=sq.n_chunks),
        grid=(sq.bsz, sq.nt),
        in_specs=[rows(wq, q_first), rows(wq, q_first + 1), rows(wv, v_first), rows(wv, v_first + 1), tab, tab, state],
        out_specs=[rows(wv, 0), state],
        out_shape=[jax.ShapeDtypeStruct((sq.m_pad, wv), BF16),
                   jax.ShapeDtypeStruct((sq.bsz, H_D, DK_D, DV_D), F32)],
        compiler_params=_cparams(2),
        name="retention",
    )(u, u, u, u, cos2, sin2, s0.astype(F32))


def _pad_rows(sq, a):
    if sq.t_pad == sq.t_len:
        return a
    a = a.reshape(sq.bsz, sq.t_len, a.shape[-1])
    return jnp.pad(a, ((0, 0), (0, sq.t_pad - sq.t_len), (0, 0))).reshape(sq.m_pad, a.shape[-1])


def _real_rows(sq, a):
    if sq.t_pad == sq.t_len:
        return a
    return a.reshape(sq.bsz, sq.t_pad, a.shape[-1])[:, :sq.t_len].reshape(sq.bsz * sq.t_len, a.shape[-1])


def _last_inputs(sq, u, first_col, width):
    assert sq.t_len >= CONV_W - 1
    u3 = u.reshape(sq.bsz, sq.t_pad, u.shape[-1])
    return u3[:, sq.t_len - (CONV_W - 1):sq.t_len, first_col:first_col + width]


def _mixer_ab(h, bsz, t_len, layer, s_hgrn, s_delta, s_dconv, p, wb):
    sq_a = _SeqLayout(bsz, t_len, CHUNK_A)
    sq_b = _SeqLayout(bsz, t_len, CHUNK_B)
    assert sq_a.t_pad == sq_b.t_pad
    hp = _pad_rows(sq_a, h)
    tm = min(sq_a.m_pad, 1024)
    u = _matmul(hp, wb['w_in0'], IN0_MAIN, F32, tm, 1024)
    gates = _matmul(hp, wb['w_in0_gates'], wb['w_in0_gates'].shape[1], F32, tm, wb['w_in0_gates'].shape[1])
    lb_all = jnp.cumsum(jax.nn.softmax(p['hgrn_lb_logits'].astype(F32), axis=0), axis=0)
    o_a, hgrn_new = _hgrn(sq_a, u, s_hgrn.astype(F32), lb_all[layer], p['hgrn_norm_w'].astype(F32))
    q, k, v = _delta_pre(sq_b, u, s_dconv, p['delta_conv_w'])
    o_b, delta_new = _delta(sq_b, q, k, v, u, gates, s_delta.astype(F32), p['delta_a_log'], p['delta_dt_bias'],
                            p['delta_norm_w'])
    dconv_new = _last_inputs(sq_b, u, 4 * H_A * DK_A, C_B)
    return _real_rows(sq_a, o_a), _real_rows(sq_b, o_b), (hgrn_new, delta_new, dconv_new)


def _mixer_cd(h, bsz, t_len, start, s_rg, s_rgconv, s_ret, p, wb):
    sq = _SeqLayout(bsz, t_len, CHUNK_D)
    hp = _pad_rows(sq, h)
    u = _matmul(hp, wb['w_in1'], IN1, F32, min(sq.m_pad, 1024), 1024)
    y_rg, rg_new = _rglru(sq, u, s_rgconv, s_rg, p, first_pos_is_zero=(start == 0))
    o_d, ret_new = _retention(sq, u, s_ret, start)
    rgconv_new = _last_inputs(sq, u, 0, RG_WIDTH)
    return _real_rows(sq, y_rg), _real_rows(sq, o_d), (rg_new, rgconv_new, ret_new)


def _trunk(x, mod_all, start, states, p, wb):
    s_hgrn, s_delta, s_dconv, s_rg, s_rgconv, s_ret = states
    bsz, t_len, _ = x.shape
    lay = _RowLayout(bsz, t_len)
    m = lay.m
    tm_mm = min(m, 1024)
    tm_ffn = min(m, 2048)
    tm_out = min(m, 1024)
    x = x.reshape(m, D_MODEL)

    mods = [lay.mod_arg(mod_all[layer]) for layer in range(DEPTH)]
    h = _modulate(lay, x, mods[0], 0)
    for layer in range(DEPTH):
        lg, lb = p['ln_g'][layer], p['ln_b'][layer]
        mid = _ffn_in(h, wb['ffn_w_in'], layer, 0, tm_ffn)
        ffn = _matmul(mid, wb['ffn_w_out'], D_MODEL, F32, tm_out, 512, lead=(layer, 0),
                      single_buffer_rows=True)
        x, h = _post_norm(lay, x, ffn, mods[layer], 0, lg[0], lb[0], 0.5, (mods[layer], 1))
        if layer % 2 == 0:
            o_1, o_2, (s_hgrn, s_delta, s_dconv) = _mixer_ab(h, bsz, t_len, layer, s_hgrn, s_delta, s_dconv, p, wb)
            w_o = wb['w_out0']
        else:
            o_1, o_2, (s_rg, s_rgconv, s_ret) = _mixer_cd(h, bsz, t_len, start, s_rg, s_rgconv, s_ret, p, wb)
            w_o = wb['w_out1']
        mix = _matmul_halves(o_1, o_2, w_o, tm_mm, 1024)
        x, h = _post_norm(lay, x, mix, mods[layer], 1, lg[1], lb[1], 1.0, (mods[layer], 2))
        mid = _ffn_in(h, wb['ffn_w_in'], layer, 1, tm_ffn)
        ffn = _matmul(mid, wb['ffn_w_out'], D_MODEL, F32, tm_out, 512, lead=(layer, 1),
                      single_buffer_rows=True)
        nxt = (mods[layer + 1], 0) if layer + 1 < DEPTH else None
        x, h = _post_norm(lay, x, ffn, mods[layer], 2, lg[2], lb[2], 0.5, nxt)
    return x.reshape(bsz, t_len, D_MODEL), (s_hgrn, s_delta, s_dconv, s_rg, s_rgconv, s_ret)


def kernel(x_prompt, x_sample, c_prompt, c_sample, state_hgrn, state_delta, state_delta_conv, state_rglru, state_rglru_conv, state_ret, ada_w, ada_b, ln_g, ln_b, ffn_w_in, ffn_w_out, w_in0, w_out0, hgrn_lb_logits, hgrn_norm_w, delta_conv_w, delta_a_log, delta_dt_bias, delta_norm_w, w_in1, w_out1, rg_conv_w, rg_conv_b, rg_wa, rg_ba, rg_wx, rg_bx, rg_lambda):
    p = {
        'ln_g': ln_g, 'ln_b': ln_b, 'hgrn_lb_logits': hgrn_lb_logits, 'hgrn_norm_w': hgrn_norm_w,
        'delta_conv_w': delta_conv_w, 'delta_a_log': delta_a_log, 'delta_dt_bias': delta_dt_bias,
        'delta_norm_w': delta_norm_w, 'rg_conv_w': rg_conv_w, 'rg_conv_b': rg_conv_b, 'rg_wa': rg_wa,
        'rg_ba': rg_ba, 'rg_wx': rg_wx, 'rg_bx': rg_bx, 'rg_lambda': rg_lambda,
    }
    wb = {
        'ffn_w_in': ffn_w_in,
        'ffn_w_out': ffn_w_out.astype(BF16),
        'w_in0': w_in0.astype(BF16),
        'w_in0_gates': _delta_gate_weights(w_in0),
        'w_out0': w_out0.astype(BF16),
        'w_in1': w_in1.astype(BF16),
        'w_out1': w_out1.astype(BF16),
    }
    nb, ns = x_prompt.shape[0], x_sample.shape[0]
    c_all = jnp.concatenate([c_prompt, c_sample], axis=0).astype(F32)
    rows = nb + ns
    rows_pad = -(-rows // 16) * 16
    c_act = jnp.pad(jax.nn.silu(c_all), ((0, rows_pad - rows), (0, 0))).astype(BF16)
    mod = _ada(c_act, ada_w, ada_b)
    mod_prompt, mod_sample = mod[:, :nb], mod[:, nb:rows]

    prompt_states = (
        jnp.zeros((nb, H_A, DK_A, DV_A), F32),
        jnp.zeros((nb, H_B, DK_B, DV_B), F32),
        jnp.zeros((nb, CONV_W - 1, C_B), F32),
        jnp.zeros((nb, RG_WIDTH), F32),
        jnp.zeros((nb, CONV_W - 1, RG_WIDTH), F32),
        jnp.zeros((nb, H_D, DK_D, DV_D), F32),
    )
    y_prompt, ps = _trunk(x_prompt, mod_prompt, 0, prompt_states, p, wb)
    sample_states = (state_hgrn, state_delta, state_delta_conv, state_rglru, state_rglru_conv, state_ret)
    y_sample, ss = _trunk(x_sample, mod_sample, PAST_LEN, sample_states, p, wb)
    return (y_prompt, y_sample) + tuple(ps) + tuple(ss)
```

```python
import functools
import math

import jax
import jax.numpy as jnp
import numpy as np
from jax import lax
from jax.experimental import pallas as pl
from jax.experimental.pallas import tpu as pltpu

D_MODEL = 4096
DEPTH = 2
PAST_LEN = 16384
MIX_HALF = D_MODEL // 2
DK_A = 128
DV_A = 128
H_A = MIX_HALF // DV_A
DK_B = 128
DV_B = 128
H_B = MIX_HALF // DV_B
CONV_W = 4
C_B = 2 * H_B * DK_B + H_B * DV_B
RG_WIDTH = MIX_HALF
RG_BW = 128
RG_BLOCKS = RG_WIDTH // RG_BW
RG_C = 8.0
DK_D = 128
DV_D = 256
H_D = MIX_HALF // DV_D
D_FF = 11008
CHUNK_A = 16
CHUNK_B = 64
CHUNK_D = 64
ROPE_BASE = 10000.0
LN_EPS = 1e-5
RMS_EPS = 1e-6
ALPHA = (2 * DEPTH) ** 0.25

IN0_SIZES = (H_A * DK_A, H_A * DK_A, H_A * DV_A, H_A * DV_A, C_B, H_B * DV_B, H_B, H_B)
IN0 = sum(IN0_SIZES)
IN0_MAIN = IN0 - 2 * H_B
IN1_SIZES = (RG_WIDTH, RG_WIDTH, H_D * DK_D, H_D * DK_D, H_D * DV_D, H_D * DV_D)
IN1 = sum(IN1_SIZES)

F32 = jnp.float32
BF16 = jnp.bfloat16

V7X_VMEM_LIMIT_BYTES = 56 * 1024 * 1024
V7X_LANES = 128
SUBLANES = 8
FFN_TN = 256
HEADS_PER_STEP = 4
GATE_BETA_LANE = 16


def _cparams(n_axes):
    return pltpu.CompilerParams(dimension_semantics=("arbitrary",) * n_axes,
                                vmem_limit_bytes=V7X_VMEM_LIMIT_BYTES)


def _dot(a, b):
    return jnp.dot(a, b, preferred_element_type=F32)


def _dot_nt(a, b):
    return lax.dot_general(a, b, (((1,), (1,)), ((), ())), preferred_element_type=F32)


def _dot_tn(a, b):
    return lax.dot_general(a, b, (((0,), (0,)), ((), ())), preferred_element_type=F32)


def _split_bf16(x, terms):
    parts = []
    for _ in range(terms - 1):
        hi = x.astype(BF16)
        parts.append(hi)
        x = x - hi.astype(F32)
    parts.append(x.astype(BF16))
    return parts


def _dot_split(a, b):
    ah, al = _split_bf16(a, 2)
    bh, bl = _split_bf16(b, 2)
    return _dot(ah, bh) + (_dot(ah, bl) + _dot(al, bh))


def _dot_select(sel, x, nt=False):
    f = _dot_nt if nt else _dot
    x1, x2, x3 = _split_bf16(x, 3)
    return f(sel, x1) + (f(sel, x2) + f(sel, x3))


def _iota2(shape, axis):
    return lax.broadcasted_iota(jnp.int32, shape, axis)


def _mm_kernel(x_ref, w_ref, o_ref):
    o_ref[...] = jnp.dot(x_ref[...], w_ref[...], preferred_element_type=F32).astype(o_ref.dtype)


def _matmul(x, w, n_cols, out_dtype, tm, tn, lead=()):
    m, k = x.shape
    assert m % tm == 0 and n_cols % tn == 0 and w.shape[len(lead)] == k
    return pl.pallas_call(
        _mm_kernel,
        grid=(m // tm, n_cols // tn),
        in_specs=[pl.BlockSpec((tm, k), lambda i, j: (i, 0)),
                  pl.BlockSpec((None,) * len(lead) + (k, tn), lambda i, j: lead + (0, j))],
        out_specs=pl.BlockSpec((tm, tn), lambda i, j: (i, j)),
        out_shape=jax.ShapeDtypeStruct((m, n_cols), out_dtype),
        compiler_params=_cparams(2),
        name="matmul",
    )(x, w)


def _mm2_kernel(xa_ref, xb_ref, wa_ref, wb_ref, o_ref):
    o_ref[...] = (jnp.dot(xa_ref[...], wa_ref[...], preferred_element_type=F32)
                  + jnp.dot(xb_ref[...], wb_ref[...], preferred_element_type=F32)).astype(o_ref.dtype)


def _matmul_halves(xa, xb, w, tm, tn):
    m, kh = xa.shape
    n = w.shape[1]
    return pl.pallas_call(
        _mm2_kernel,
        grid=(m // tm, n // tn),
        in_specs=[pl.BlockSpec((tm, kh), lambda i, j: (i, 0)), pl.BlockSpec((tm, kh), lambda i, j: (i, 0)),
                  pl.BlockSpec((kh, tn), lambda i, j: (0, j)), pl.BlockSpec((kh, tn), lambda i, j: (1, j))],
        out_specs=pl.BlockSpec((tm, tn), lambda i, j: (i, j)),
        out_shape=jax.ShapeDtypeStruct((m, n), F32),
        compiler_params=_cparams(2),
        name="matmul_halves",
    )(xa, xb, w, w)


def _swiglu_kernel(x_ref, wg_ref, wu_ref, o_ref):
    x = x_ref[...]
    gate = jnp.dot(x, wg_ref[...].astype(BF16), preferred_element_type=F32)
    up = jnp.dot(x, wu_ref[...].astype(BF16), preferred_element_type=F32)
    o_ref[...] = (gate * jax.nn.sigmoid(gate) * up).astype(o_ref.dtype)


def _ffn_in(x, ffn_w_in, layer, half, tm):
    m, k = x.shape
    n_tiles = D_FF // FFN_TN
    return pl.pallas_call(
        _swiglu_kernel,
        grid=(m // tm, n_tiles),
        in_specs=[pl.BlockSpec((tm, k), lambda i, j: (i, 0), pipeline_mode=pl.Buffered(1)),
                  pl.BlockSpec((None, None, k, FFN_TN), lambda i, j: (layer, half, 0, j)),
                  pl.BlockSpec((None, None, k, FFN_TN), lambda i, j: (layer, half, 0, j + n_tiles))],
        out_specs=pl.BlockSpec((tm, FFN_TN), lambda i, j: (i, j)),
        out_shape=jax.ShapeDtypeStruct((m, D_FF), BF16),
        compiler_params=_cparams(2),
        name="ffn_in",
    )(x, ffn_w_in, ffn_w_in)


def _ada_kernel(c_ref, w_ref, b_ref, o_ref):
    o_ref[...] = jnp.dot(c_ref[...], w_ref[...].astype(BF16), preferred_element_type=F32) + b_ref[...]


def _ada(c_act, ada_w, ada_b, tn=512):
    r = c_act.shape[0]
    n = ada_w.shape[2]
    return pl.pallas_call(
        _ada_kernel,
        grid=(DEPTH, n // tn),
        in_specs=[pl.BlockSpec((r, D_MODEL), lambda l, j: (0, 0)),
                  pl.BlockSpec((None, D_MODEL, tn), lambda l, j: (l, 0, j)),
                  pl.BlockSpec((None, 1, tn), lambda l, j: (l, 0, j))],
        out_specs=pl.BlockSpec((None, r, tn), lambda l, j: (l, 0, j)),
        out_shape=jax.ShapeDtypeStruct((DEPTH, r, n), F32),
        compiler_params=_cparams(2),
        name="ada",
    )(c_act, ada_w, ada_b.reshape(DEPTH, 1, n))


def _per_row(mod_ref, rows, rep):
    v = mod_ref[...]
    if rep == 1:
        return v
    sel = (_iota2((rows, rows // rep), 0) // rep == _iota2((rows, rows // rep), 1)).astype(BF16)
    return _dot_select(sel, v)


def _modulate_kernel(x_ref, shift_ref, scale_ref, h_ref, *, rep):
    rows = x_ref.shape[0]
    h_ref[...] = (x_ref[...] * (1.0 + _per_row(scale_ref, rows, rep)) + _per_row(shift_ref, rows, rep)).astype(h_ref.dtype)


def _post_norm_body(x_ref, out_ref, gate_ref, g_ref, b_ref, rho, rep):
    z = ALPHA * x_ref[...] + rho * _per_row(gate_ref, x_ref.shape[0], rep) * out_ref[...]
    mu = jnp.mean(z, axis=-1, keepdims=True)
    zc = z - mu
    var = jnp.mean(zc * zc, axis=-1, keepdims=True)
    return zc * lax.rsqrt(var + LN_EPS) * g_ref[...] + b_ref[...]


def _post_norm_mod_kernel(x_ref, out_ref, gate_ref, g_ref, b_ref, shift_ref, scale_ref, xn_ref, h_ref, *, rho, rep):
    rows = x_ref.shape[0]
    xn = _post_norm_body(x_ref, out_ref, gate_ref, g_ref, b_ref, rho, rep)
    xn_ref[...] = xn
    h_ref[...] = (xn * (1.0 + _per_row(scale_ref, rows, rep)) + _per_row(shift_ref, rows, rep)).astype(h_ref.dtype)


def _post_norm_kernel(x_ref, out_ref, gate_ref, g_ref, b_ref, xn_ref, *, rho, rep):
    xn_ref[...] = _post_norm_body(x_ref, out_ref, gate_ref, g_ref, b_ref, rho, rep)


class _RowLayout:
    def __init__(self, bsz, t_len):
        self.bsz, self.t_len = bsz, t_len
        self.m = bsz * t_len
        self.per_tile = t_len >= 256
        self.tm = min(self.m, 256)
        self.rep = 1 if self.per_tile else t_len
        assert self.per_tile or (self.tm % t_len == 0 and (self.tm // t_len) % SUBLANES == 0)

    def mod_arg(self, mod_layer):
        if self.per_tile:
            return mod_layer.reshape(self.bsz, 1, 9 * D_MODEL)
        return mod_layer

    def mod_spec(self, col):
        if self.per_tile:
            per = self.t_len // self.tm
            return pl.BlockSpec((None, 1, D_MODEL), lambda i: (i // per, 0, col))
        return pl.BlockSpec((self.tm // self.t_len, D_MODEL), lambda i: (i, col))

    def row_spec(self):
        return pl.BlockSpec((self.tm, D_MODEL), lambda i: (i, 0))

    @staticmethod
    def vec_spec():
        return pl.BlockSpec((1, D_MODEL), lambda i: (0, 0))


def _modulate(lay, x, mods, sub):
    return pl.pallas_call(
        functools.partial(_modulate_kernel, rep=lay.rep),
        grid=(lay.m // lay.tm,),
        in_specs=[lay.row_spec(), lay.mod_spec(3 * sub), lay.mod_spec(3 * sub + 1)],
        out_specs=lay.row_spec(),
        out_shape=jax.ShapeDtypeStruct((lay.m, D_MODEL), BF16),
        compiler_params=_cparams(1),
        name="modulate",
    )(x, mods, mods)


def _post_norm(lay, x, out, mods, sub, g, b, rho, nxt):
    g2, b2 = g.reshape(1, D_MODEL), b.reshape(1, D_MODEL)
    common = dict(grid=(lay.m // lay.tm,), compiler_params=_cparams(1))
    base_specs = [lay.row_spec(), lay.row_spec(), lay.mod_spec(3 * sub + 2), lay.vec_spec(), lay.vec_spec()]
    if nxt is None:
        return pl.pallas_call(
            functools.partial(_post_norm_kernel, rho=rho, rep=lay.rep),
            in_specs=base_specs,
            out_specs=lay.row_spec(),
            out_shape=jax.ShapeDtypeStruct((lay.m, D_MODEL), F32),
            name="post_norm", **common,
        )(x, out, mods, g2, b2), None
    nxt_mods, nxt_sub = nxt
    return pl.pallas_call(
        functools.partial(_post_norm_mod_kernel, rho=rho, rep=lay.rep),
        in_specs=base_specs + [lay.mod_spec(3 * nxt_sub), lay.mod_spec(3 * nxt_sub + 1)],
        out_specs=[lay.row_spec(), lay.row_spec()],
        out_shape=[jax.ShapeDtypeStruct((lay.m, D_MODEL), F32), jax.ShapeDtypeStruct((lay.m, D_MODEL), BF16)],
        name="post_norm_mod", **common,
    )(x, out, mods, g2, b2, nxt_mods, nxt_mods)


def _sigmoid(x):
    return 1.0 / (1.0 + jnp.exp(-x))


def _silu(x):
    return x * _sigmoid(x)


def _softplus(x):
    return jnp.maximum(x, 0.0) + jnp.log1p(jnp.exp(-jnp.abs(x)))


def _neg_expm1(x):
    return -jnp.tanh(0.5 * x) * (jnp.exp(x) + 1.0)


class _SeqLayout:
    def __init__(self, bsz, t_len, chunk_max):
        self.bsz, self.t_len = bsz, t_len
        blk = math.gcd(t_len, chunk_max)
        if blk % SUBLANES == 0:
            self.t_pad, self.chunk, self.valid = t_len, blk, blk
        else:
            assert blk == t_len and t_len < SUBLANES
            self.t_pad, self.chunk, self.valid = SUBLANES, SUBLANES, t_len
        self.tb = min(self.t_pad, 256)
        assert self.t_pad % self.tb == 0 and self.tb % self.chunk == 0
        self.nt = self.t_pad // self.tb
        self.n_chunks = self.tb // self.chunk
        self.m_pad = bsz * self.t_pad


def _row_mask(chunk, valid):
    return None if valid == chunk else (_iota2((chunk, 1), 0) < valid)


def _hgrn_kernel(q_ref, f_ref, i_ref, g_ref, s0_ref, lb_ref, nw_ref, o_ref, s_ref,
                 oi_ref, qd_ref, ke_ref, vb_ref, ge_ref, st_ref, *, tb, chunk, valid, chunks_per_req, carried):
    hg = HEADS_PER_STEP
    n_chunks = tb // chunk
    shift = int(math.log2(chunk))
    assert 1 << shift == chunk

    if carried:
        @pl.when(pl.program_id(2) == 0)
        def _():
            for h in range(hg):
                st_ref[h] = s0_ref[0, h].T
    else:
        s_ref[...] = s0_ref[...]

    lb = lb_ref[...]
    nw = nw_ref[...]
    row = _iota2((tb, tb), 0)
    col = _iota2((tb, tb), 1)
    same = (row >> shift) == (col >> shift)
    incl = jnp.logical_and(same, row >= col)

    f = lb + (1.0 - lb) * _sigmoid(f_ref[...])
    log_f = jnp.log(f)
    k = 1.0 - f
    v = i_ref[...]
    if valid != chunk:
        real = (_iota2((tb, 1), 0) & (chunk - 1)) < valid
        log_f = jnp.where(real, log_f, 0.0)
        v = jnp.where(real, v, 0.0)
    cum = _dot_select(incl.astype(BF16), log_f)
    cum_tot = _dot_select(same.astype(BF16), log_f)
    g_end = jnp.exp(cum_tot)
    ge_ref[...] = g_end
    q_dec = (_silu(q_ref[...]) * jnp.exp(cum)).astype(BF16)
    k_inv = (k * jnp.exp(-cum)).astype(BF16)
    k_end = k * jnp.exp(cum_tot - cum)
    vb = v.astype(BF16)
    eye_dk = (_iota2((DK_A, DK_A), 0) == _iota2((DK_A, DK_A), 1)).astype(BF16)
    g_cols = []
    for h in range(hg):
        sl = slice(h * DK_A, (h + 1) * DK_A)
        scores = jnp.where(incl, _dot_nt(q_dec[:, sl], k_inv[:, sl]), 0.0)
        oi_ref[h] = _dot(scores.astype(BF16), vb[:, sl])
        qd_ref[h] = q_dec[:, sl].astype(qd_ref.dtype)
        ke_ref[h] = k_end[:, sl].astype(ke_ref.dtype)
        vb_ref[h] = v[:, sl].astype(vb_ref.dtype)
        if not carried:
            g_cols.append(_dot_select(eye_dk, g_end[:, sl], nt=True))

    def finish(h, r, o):
        sl = slice(h * DK_A, (h + 1) * DK_A)
        ms = jnp.mean(o * o, axis=-1, keepdims=True)
        o_ref[r, sl] = (o * lax.rsqrt(ms + RMS_EPS) * nw[:, sl] * _silu(g_ref[r, sl])).astype(o_ref.dtype)

    if carried:
        def body(j, carry):
            r = pl.ds(pl.multiple_of(j * chunk, chunk), chunk)
            g_rows = ge_ref[r, :]
            for h in range(hg):
                sl = slice(h * DK_A, (h + 1) * DK_A)
                st = st_ref[h]
                o = oi_ref[h, r, :] + _dot_nt(qd_ref[h, r, :].astype(BF16), st.astype(BF16))
                st_ref[h] = st * g_rows[0:1, sl] + _dot_tn(vb_ref[h, r, :].astype(BF16), ke_ref[h, r, :].astype(BF16))
                finish(h, r, o)
            return carry

        lax.fori_loop(0, n_chunks, body, 0, unroll=4)

        @pl.when(pl.program_id(2) == pl.num_programs(2) - 1)
        def _():
            for h in range(hg):
                s_ref[0, h] = st_ref[h].T
    else:
        for j in range(n_chunks):
            r = slice(j * chunk, (j + 1) * chunk)
            req = j // chunks_per_req
            for h in range(hg):
                state = s_ref[req, h]
                o = oi_ref[h, r, :] + _dot(qd_ref[h, r, :].astype(BF16), state.astype(BF16))
                s_ref[req, h] = (g_cols[h][:, j * chunk:j * chunk + 1] * state
                                 + _dot_tn(ke_ref[h, r, :].astype(BF16), vb_ref[h, r, :].astype(BF16)))
                finish(h, r, o)


def _hgrn(sq, u, s0, lb, norm_w):
    hg = HEADS_PER_STEP
    wl = hg * DK_A
    nsec = (H_A * DK_A) // wl
    if sq.valid == sq.chunk:
        tb, reqs, nb, nt, carried = sq.tb, 1, sq.bsz, sq.nt, True
    else:
        assert sq.t_pad == sq.chunk
        reqs = math.gcd(sq.bsz, 128 // sq.t_pad)
        tb, nb, nt, carried = reqs * sq.t_pad, sq.bsz // reqs, 1, False
    chunks_per_req = (tb // sq.chunk) // reqs
    row_dt = BF16 if sq.chunk % (2 * SUBLANES) == 0 else F32

    def sec(s):
        return pl.BlockSpec((tb, wl), lambda b, g, t: (b * nt + t, s * nsec + g))

    state_spec = pl.BlockSpec((reqs, hg, DK_A, DV_A), lambda b, g, t: (b, g, 0, 0))
    vec_spec = pl.BlockSpec((1, wl), lambda b, g, t: (0, g))
    return pl.pallas_call(
        functools.partial(_hgrn_kernel, tb=tb, chunk=sq.chunk, valid=sq.valid, chunks_per_req=chunks_per_req,
                          carried=carried),
        grid=(nb, nsec, nt),
        in_specs=[sec(0), sec(1), sec(2), sec(3), state_spec, vec_spec, vec_spec],
        out_specs=[pl.BlockSpec((tb, wl), lambda b, g, t: (b * nt + t, g)), state_spec],
        out_shape=[jax.ShapeDtypeStruct((sq.m_pad, H_A * DV_A), BF16),
                   jax.ShapeDtypeStruct((sq.bsz, H_A, DK_A, DV_A), F32)],
        scratch_shapes=[pltpu.VMEM((hg, tb, DV_A), F32), pltpu.VMEM((hg, tb, DK_A), row_dt),
                        pltpu.VMEM((hg, tb, DK_A), row_dt), pltpu.VMEM((hg, tb, DV_A), row_dt),
                        pltpu.VMEM((tb, wl), F32), pltpu.VMEM((hg, DV_A, DK_A), F32)],
        compiler_params=_cparams(3),
        name="hgrn",
    )(u, u, u, u, s0, lb.reshape(1, -1), norm_w.reshape(1, -1))


def _delta_pre_kernel(qr_ref, kr_ref, vr_ref, cs_ref, w_ref, q_ref, k_ref, v_ref, carry_ref, *, tb):
    t = pl.program_id(1)
    width = H_B * DK_B

    @pl.when(t == 0)
    def _():
        for s in range(3):
            carry_ref[s] = cs_ref[:, s * width:(s + 1) * width]

    for s, (x_ref, y_ref) in enumerate(((qr_ref, q_ref), (kr_ref, k_ref), (vr_ref, v_ref))):
        w = w_ref[:, s * width:(s + 1) * width]
        x = x_ref[...]
        xx = jnp.concatenate([carry_ref[s], x], axis=0)
        y = w[CONV_W - 1:CONV_W] * x
        for j in range(1, CONV_W):
            y = y + w[CONV_W - 1 - j:CONV_W - j] * xx[SUBLANES - j:SUBLANES - j + tb]
        carry_ref[s] = x[tb - SUBLANES:tb]
        y = _silu(y)
        if s == 2:
            y_ref[...] = y
        else:
            scale = DK_B ** -0.5 if s == 0 else 1.0
            for h in range(H_B):
                sl = slice(h * DK_B, (h + 1) * DK_B)
                yh = y[:, sl]
                ss = jnp.sum(yh * yh, axis=-1, keepdims=True)
                y_ref[:, sl] = yh * (lax.rsqrt(ss + RMS_EPS) * scale)


def _delta_pre(sq, u, conv_state, conv_w):
    width = H_B * DK_B
    first = (4 * H_A * DK_A) // width

    def sec(s):
        return pl.BlockSpec((sq.tb, width), lambda b, t: (b * sq.nt + t, first + s))

    out_spec = pl.BlockSpec((sq.tb, width), lambda b, t: (b * sq.nt + t, 0))
    cs = jnp.pad(conv_state.astype(F32), ((0, 0), (SUBLANES - (CONV_W - 1), 0), (0, 0)))
    return pl.pallas_call(
        functools.partial(_delta_pre_kernel, tb=sq.tb),
        grid=(sq.bsz, sq.nt),
        in_specs=[sec(0), sec(1), sec(2),
                  pl.BlockSpec((None, SUBLANES, C_B), lambda b, t: (b, 0, 0)),
                  pl.BlockSpec((CONV_W, C_B), lambda b, t: (0, 0))],
        out_specs=[out_spec, out_spec, out_spec],
        out_shape=[jax.ShapeDtypeStruct((sq.m_pad, width), F32)] * 3,
        scratch_shapes=[pltpu.VMEM((3, SUBLANES, width), F32)],
        compiler_params=_cparams(2),
        name="delta_pre",
    )(u, u, u, cs, conv_w.astype(F32))


def _delta_kernel(q_ref, k_ref, v_ref, z_ref, gt_ref, s0_ref, alog_ref, dtb_ref, nw_ref, o_ref, s_ref,
                  wv_ref, wk_ref, qd_ref, ke_ref, qk_ref, u_ref, ge_ref, *, tb, chunk, valid, chunks_per_req, carried):
    hg = HEADS_PER_STEP
    n_chunks = tb // chunk
    shift = int(math.log2(chunk))
    assert 1 << shift == chunk

    if carried:
        @pl.when(pl.program_id(2) == 0)
        def _():
            s_ref[...] = s0_ref[...]
    else:
        s_ref[...] = s0_ref[...]

    neg_a = -jnp.exp(alog_ref[...])
    dtb = dtb_ref[...]
    nw = nw_ref[...]
    row = _iota2((tb, tb), 0)
    col = _iota2((tb, tb), 1)
    same = (row >> shift) == (col >> shift)
    incl = jnp.logical_and(same, row >= col)
    strict = jnp.logical_and(same, row > col)
    tri_bd = incl.astype(BF16)
    ones_bd = same.astype(BF16)
    eye = (row == col).astype(F32)
    pick = (_iota2((SUBLANES, V7X_LANES), 0) == _iota2((SUBLANES, V7X_LANES), 1)).astype(BF16)
    n_sq = max(shift - 1, 0)

    gt = gt_ref[...]
    log_g = neg_a * _softplus(gt + dtb)
    beta_all = _sigmoid(gt)
    if valid != chunk:
        real = (_iota2((tb, 1), 0) & (chunk - 1)) < valid
        log_g = jnp.where(real, log_g, 0.0)
        beta_all = jnp.where(real, beta_all, 0.0)
    cum = _dot_select(tri_bd, log_g)
    cum_tot = _dot_select(ones_bd, log_g)
    cum_t = _dot_select(pick, cum, nt=True)
    ge_ref[...] = jnp.exp(cum_tot)
    k_end_scale = jnp.exp(cum_tot - cum)
    e_all = jnp.exp(cum)

    for h in range(hg):
        sl = slice(h * DK_B, (h + 1) * DK_B)
        c_col = cum[:, h:h + 1]
        c_row = cum_t[h:h + 1, :]
        beta = beta_all[:, GATE_BETA_LANE + h:GATE_BETA_LANE + h + 1]
        decay = jnp.where(incl, jnp.exp(jnp.where(incl, c_col - c_row, 0.0)), 0.0)
        qh, kh, vh = q_ref[:, sl], k_ref[:, sl], v_ref[:, sl]
        qb, kb = qh.astype(BF16), kh.astype(BF16)
        nmat = jnp.where(strict, beta * _dot_nt(kb, kb) * decay, 0.0)
        inv = eye - nmat
        power = nmat
        for _ in range(n_sq):
            power = _dot_split(power, power)
            inv = inv + _dot_split(inv, power)
        e_col = e_all[:, h:h + 1]
        rhs = jnp.concatenate([beta * vh, (beta * e_col) * kh], axis=1)
        w = _dot_split(inv, rhs)
        wv_ref[h] = w[:, :DV_B]
        wk_ref[h] = w[:, DV_B:].astype(wk_ref.dtype)
        qd_ref[h] = (qh * e_col).astype(qd_ref.dtype)
        ke_ref[h] = (kh * k_end_scale[:, h:h + 1]).astype(ke_ref.dtype)
        qk_ref[h] = (_dot_nt(qb, kb) * decay).astype(qk_ref.dtype)
        u_ref[h] = jnp.zeros((tb, DV_B), u_ref.dtype)

    def walk(r, req):
        g_rows = ge_ref[r, :]
        for h in range(hg):
            sl = slice(h * DK_B, (h + 1) * DK_B)
            state = s_ref[req, h]
            sb = state.astype(BF16)
            both = _dot(jnp.concatenate([wk_ref[h, r, :], qd_ref[h, r, :]], axis=0).astype(BF16), sb)
            u = wv_ref[h, r, :] - both[:chunk]
            ub = u.astype(BF16)
            u_ref[h, r, :] = u.astype(u_ref.dtype)
            o = both[chunk:] + _dot(qk_ref[h, r, :].astype(BF16), u_ref[h].astype(BF16))
            s_ref[req, h] = g_rows[0:1, h:h + 1] * state + _dot_tn(ke_ref[h, r, :].astype(BF16), ub)
            ms = jnp.mean(o * o, axis=-1, keepdims=True)
            o_ref[r, sl] = (o * lax.rsqrt(ms + RMS_EPS) * nw * _silu(z_ref[r, sl])).astype(o_ref.dtype)

    if carried:
        def body(j, carry):
            walk(pl.ds(pl.multiple_of(j * chunk, chunk), chunk), 0)
            return carry

        lax.fori_loop(0, n_chunks, body, 0)
    else:
        for j in range(n_chunks):
            walk(slice(j * chunk, (j + 1) * chunk), j // chunks_per_req)


def _delta_gate_lanes(v, fill):
    hg = HEADS_PER_STEP
    v = v.astype(F32).reshape(H_B // hg, 1, hg)
    return jnp.pad(v, ((0, 0), (0, 0), (0, V7X_LANES - hg)), constant_values=fill)


def _delta_gate_weights(w_in0):
    hg = HEADS_PER_STEP
    ab = w_in0[:, IN0_MAIN:IN0_MAIN + H_B]
    bb = w_in0[:, IN0_MAIN + H_B:]
    tiles = []
    for g in range(H_B // hg):
        tile = jnp.zeros((D_MODEL, V7X_LANES), w_in0.dtype)
        tile = tile.at[:, :hg].set(ab[:, g * hg:(g + 1) * hg])
        tile = tile.at[:, GATE_BETA_LANE:GATE_BETA_LANE + hg].set(bb[:, g * hg:(g + 1) * hg])
        tiles.append(tile)
    return jnp.concatenate(tiles, axis=1).astype(BF16)


def _delta(sq, q, k, v, u, gates, s0, a_log, dt_bias, norm_w):
    hg = HEADS_PER_STEP
    wl = hg * DK_B
    ngrp = H_B // hg
    z_first = (4 * H_A * DK_A + C_B) // wl
    if sq.valid == sq.chunk:
        tb, reqs, nb, nt, carried = sq.tb, 1, sq.bsz, sq.nt, True
    else:
        assert sq.t_pad == sq.chunk
        reqs = math.gcd(sq.bsz, 128 // sq.t_pad)
        tb, nb, nt, carried = reqs * sq.t_pad, sq.bsz // reqs, 1, False
    chunks_per_req = (tb // sq.chunk) // reqs
    row_dt = BF16 if sq.chunk % (2 * SUBLANES) == 0 else F32

    def rows(first):
        return pl.BlockSpec((tb, wl), lambda b, g, t: (b * nt + t, first + g))

    state_spec = pl.BlockSpec((reqs, hg, DK_B, DV_B), lambda b, g, t: (b, g, 0, 0))
    lane_spec = pl.BlockSpec((None, 1, V7X_LANES), lambda b, g, t: (g, 0, 0))
    return pl.pallas_call(
        functools.partial(_delta_kernel, tb=tb, chunk=sq.chunk, valid=sq.valid, chunks_per_req=chunks_per_req,
                          carried=carried),
        grid=(nb, ngrp, nt),
        in_specs=[rows(0), rows(0), rows(0), rows(z_first),
                  pl.BlockSpec((tb, V7X_LANES), lambda b, g, t: (b * nt + t, g)),
                  state_spec, lane_spec, lane_spec,
                  pl.BlockSpec((1, DV_B), lambda b, g, t: (0, 0))],
        out_specs=[rows(0), state_spec],
        out_shape=[jax.ShapeDtypeStruct((sq.m_pad, H_B * DV_B), BF16),
                   jax.ShapeDtypeStruct((sq.bsz, H_B, DK_B, DV_B), F32)],
        scratch_shapes=[pltpu.VMEM((hg, tb, DV_B), F32), pltpu.VMEM((hg, tb, DK_B), row_dt),
                        pltpu.VMEM((hg, tb, DK_B), row_dt), pltpu.VMEM((hg, tb, DK_B), row_dt),
                        pltpu.VMEM((hg, tb, tb), row_dt), pltpu.VMEM((hg, tb, DV_B), row_dt),
                        pltpu.VMEM((tb, V7X_LANES), F32)],
        compiler_params=_cparams(3),
        name="delta",
    )(q, k, v, u, gates, s0, _delta_gate_lanes(a_log, 0.0), _delta_gate_lanes(dt_bias, 0.0),
      norm_w.astype(F32).reshape(1, DV_B))


def _rglru_kernel(x_ref, gate_ref, cs_ref, h0_ref, cw_ref, cb_ref, wa_ref, wx_ref, ba_ref, bx_ref, lam_ref,
                  y_ref, hl_ref, carry_ref, h_ref, a_ref, b_ref, *, tb, valid_rows, first_pos_is_zero):
    t = pl.program_id(1)

    @pl.when(t == 0)
    def _():
        carry_ref[...] = cs_ref[...]
        h_ref[...] = h0_ref[...]

    w = cw_ref[...]
    x = x_ref[...]
    xx = jnp.concatenate([carry_ref[...], x], axis=0)
    xr = w[CONV_W - 1:CONV_W] * x + cb_ref[...]
    for j in range(1, CONV_W):
        xr = xr + w[CONV_W - 1 - j:CONV_W - j] * xx[SUBLANES - j:SUBLANES - j + tb]
    carry_ref[...] = x[tb - SUBLANES:tb]
    neg_c_sp = -RG_C * _softplus(-lam_ref[...])
    for n in range(RG_BLOCKS):
        sl = slice(n * RG_BW, (n + 1) * RG_BW)
        xn = xr[:, sl]
        xb = xn.astype(BF16)
        r = _sigmoid(_dot(xb, wa_ref[n]) + ba_ref[:, sl])
        i = _sigmoid(_dot(xb, wx_ref[n]) + bx_ref[:, sl])
        log_a = neg_c_sp[:, sl] * r
        mult = jnp.sqrt(_neg_expm1(2.0 * log_a))
        if first_pos_is_zero:
            is_first = jnp.logical_and(t == 0, _iota2((tb, 1), 0) == 0)
            mult = jnp.where(is_first, 1.0, mult)
        a_ref[:, sl] = jnp.exp(log_a)
        b_ref[:, sl] = mult * i * xn

    def step(s, h):
        h = a_ref[pl.ds(s, 1), :] * h + b_ref[pl.ds(s, 1), :]
        b_ref[pl.ds(s, 1), :] = h
        return h

    h_ref[...] = lax.fori_loop(0, valid_rows, step, h_ref[...])
    y_ref[...] = (b_ref[...] * jax.nn.gelu(gate_ref[...], approximate=True)).astype(y_ref.dtype)

    @pl.when(t == pl.num_programs(1) - 1)
    def _():
        hl_ref[...] = h_ref[...]


def _rglru(sq, u, conv_state, h0, p, first_pos_is_zero):
    def sec(s):
        return pl.BlockSpec((sq.tb, RG_WIDTH), lambda b, t: (b * sq.nt + t, s))

    vec = pl.BlockSpec((1, RG_WIDTH), lambda b, t: (0, 0))
    blk = pl.BlockSpec((RG_BLOCKS, RG_BW, RG_BW), lambda b, t: (0, 0, 0))
    state = pl.BlockSpec((None, 1, RG_WIDTH), lambda b, t: (b, 0, 0))
    cs = jnp.pad(conv_state.astype(F32), ((0, 0), (SUBLANES - (CONV_W - 1), 0), (0, 0)))
    valid_rows = sq.tb if sq.valid == sq.chunk else sq.valid
    y, h_last = pl.pallas_call(
        functools.partial(_rglru_kernel, tb=sq.tb, valid_rows=valid_rows, first_pos_is_zero=first_pos_is_zero),
        grid=(sq.bsz, sq.nt),
        in_specs=[sec(0), sec(1),
                  pl.BlockSpec((None, SUBLANES, RG_WIDTH), lambda b, t: (b, 0, 0)), state,
                  pl.BlockSpec((CONV_W, RG_WIDTH), lambda b, t: (0, 0)), vec, blk, blk, vec, vec, vec],
        out_specs=[pl.BlockSpec((sq.tb, RG_WIDTH), lambda b, t: (b * sq.nt + t, 0)), state],
        out_shape=[jax.ShapeDtypeStruct((sq.m_pad, RG_WIDTH), BF16),
                   jax.ShapeDtypeStruct((sq.bsz, 1, RG_WIDTH), F32)],
        scratch_shapes=[pltpu.VMEM((SUBLANES, RG_WIDTH), F32), pltpu.VMEM((1, RG_WIDTH), F32),
                        pltpu.VMEM((sq.tb, RG_WIDTH), F32), pltpu.VMEM((sq.tb, RG_WIDTH), F32)],
        compiler_params=_cparams(2),
        name="rglru",
    )(u, u, cs, h0.astype(F32).reshape(sq.bsz, 1, RG_WIDTH), p['rg_conv_w'].astype(F32),
      p['rg_conv_b'].astype(F32).reshape(1, -1), p['rg_wa'].astype(BF16), p['rg_wx'].astype(BF16),
      p['rg_ba'].astype(F32).reshape(1, -1), p['rg_bx'].astype(F32).reshape(1, -1),
      p['rg_lambda'].astype(F32).reshape(1, -1))
    return y, h_last.reshape(sq.bsz, RG_WIDTH)


def _ret_kernel(q_ref, k_ref, v_ref, g_ref, cos_ref, sin_ref, s0_ref, o_ref, s_ref, *, chunk, valid, n_chunks):
    t = pl.program_id(1)

    @pl.when(t == 0)
    def _():
        s_ref[...] = s0_ref[...]

    row = _iota2((chunk, chunk), 0)
    col = _iota2((chunk, chunk), 1)
    causal = row >= col
    seen = jnp.minimum(_iota2((chunk, 1), 0) + 1, valid).astype(F32)
    dist = (row - col).astype(F32)
    mask = _row_mask(chunk, valid)
    log_gamma = [math.log1p(-2.0 ** (-5.0 - h)) for h in range(H_D)]

    def body(c, carry):
        r = pl.ds(pl.multiple_of(c * chunk, chunk), chunk)
        cos = cos_ref[r, :]
        sin = sin_ref[r, :]
        for h in range(H_D):
            lg = log_gamma[h]
            slk = slice(h * DK_D, (h + 1) * DK_D)
            slv = slice(h * DV_D, (h + 1) * DV_D)
            qh, kh = q_ref[r, slk], k_ref[r, slk]
            qr = qh * cos + pltpu.roll(qh, DK_D // 2, axis=1) * sin
            kr = (kh * cos + pltpu.roll(kh, DK_D // 2, axis=1) * sin) * (DK_D ** -0.5)
            vh = v_ref[r, slv]
            if mask is not None:
                vh = jnp.where(mask, vh, 0.0)
            vb = vh.astype(BF16)
            scores = jnp.where(causal, _dot_nt(qr.astype(BF16), kr.astype(BF16)) * jnp.exp(dist * lg), 0.0)
            state = s_ref[h]
            o = _dot(scores.astype(BF16), vb) + _dot((qr * jnp.exp(seen * lg)).astype(BF16), state.astype(BF16))
            k_end = (kr * jnp.exp((valid - seen) * lg)).astype(BF16)
            s_ref[h] = math.exp(valid * lg) * state + _dot_tn(k_end, vb)
            mu = jnp.mean(o, axis=-1, keepdims=True)
            oc = o - mu
            var = jnp.mean(oc * oc, axis=-1, keepdims=True)
            o_ref[r, slv] = (oc * lax.rsqrt(var + LN_EPS) * _silu(g_ref[r, slv])).astype(o_ref.dtype)
        return carry

    lax.fori_loop(0, n_chunks, body, 0)


def _rotary_tables(sq, start):
    half = DK_D // 2
    inv = ROPE_BASE ** (-jnp.arange(half, dtype=F32) / half)
    pos = start + jnp.arange(sq.t_pad, dtype=jnp.int32)
    ang = pos.astype(F32)[:, None] * inv
    cos, sin = jnp.cos(ang), jnp.sin(ang)
    cos2 = jnp.concatenate([cos, cos], axis=-1)
    sin2 = jnp.concatenate([-sin, sin], axis=-1)
    return cos2, sin2


def _retention(sq, u, s0, start):
    wq = H_D * DK_D
    wv = H_D * DV_D
    q_first = (2 * RG_WIDTH) // wq
    v_first = (2 * RG_WIDTH + 2 * wq) // wv

    def rows(width, first):
        return pl.BlockSpec((sq.tb, width), lambda b, t: (b * sq.nt + t, first))

    tab = pl.BlockSpec((sq.tb, DK_D), lambda b, t: (t, 0))
    state = pl.BlockSpec((None, H_D, DK_D, DV_D), lambda b, t: (b, 0, 0, 0))
    cos2, sin2 = _rotary_tables(sq, start)
    return pl.pallas_call(
        functools.partial(_ret_kernel, chunk=sq.chunk, valid=sq.valid, n_chunks=sq.n_chunks),
        grid=(sq.bsz, sq.nt),
        in_specs=[rows(wq, q_first), rows(wq, q_first + 1), rows(wv, v_first), rows(wv, v_first + 1), tab, tab, state],
        out_specs=[rows(wv, 0), state],
        out_shape=[jax.ShapeDtypeStruct((sq.m_pad, wv), BF16),
                   jax.ShapeDtypeStruct((sq.bsz, H_D, DK_D, DV_D), F32)],
        compiler_params=_cparams(2),
        name="retention",
    )(u, u, u, u, cos2, sin2, s0.astype(F32))


def _pad_rows(sq, a):
    if sq.t_pad == sq.t_len:
        return a
    a = a.reshape(sq.bsz, sq.t_len, a.shape[-1])
    return jnp.pad(a, ((0, 0), (0, sq.t_pad - sq.t_len), (0, 0))).reshape(sq.m_pad, a.shape[-1])


def _real_rows(sq, a):
    if sq.t_pad == sq.t_len:
        return a
    return a.reshape(sq.bsz, sq.t_pad, a.shape[-1])[:, :sq.t_len].reshape(sq.bsz * sq.t_len, a.shape[-1])


def _last_inputs(sq, u, first_col, width):
    assert sq.t_len >= CONV_W - 1
    u3 = u.reshape(sq.bsz, sq.t_pad, u.shape[-1])
    return u3[:, sq.t_len - (CONV_W - 1):sq.t_len, first_col:first_col + width]


def _mixer_ab(h, bsz, t_len, layer, s_hgrn, s_delta, s_dconv, p, wb):
    sq_a = _SeqLayout(bsz, t_len, CHUNK_A)
    sq_b = _SeqLayout(bsz, t_len, CHUNK_B)
    assert sq_a.t_pad == sq_b.t_pad
    hp = _pad_rows(sq_a, h)
    tm = min(sq_a.m_pad, 1024)
    u = _matmul(hp, wb['w_in0'], IN0_MAIN, F32, tm, 1024)
    gates = _matmul(hp, wb['w_in0_gates'], wb['w_in0_gates'].shape[1], F32, tm, wb['w_in0_gates'].shape[1])
    lb_all = jnp.cumsum(jax.nn.softmax(p['hgrn_lb_logits'].astype(F32), axis=0), axis=0)
    o_a, hgrn_new = _hgrn(sq_a, u, s_hgrn.astype(F32), lb_all[layer], p['hgrn_norm_w'].astype(F32))
    q, k, v = _delta_pre(sq_b, u, s_dconv, p['delta_conv_w'])
    o_b, delta_new = _delta(sq_b, q, k, v, u, gates, s_delta.astype(F32), p['delta_a_log'], p['delta_dt_bias'],
                            p['delta_norm_w'])
    dconv_new = _last_inputs(sq_b, u, 4 * H_A * DK_A, C_B)
    return _real_rows(sq_a, o_a), _real_rows(sq_b, o_b), (hgrn_new, delta_new, dconv_new)


def _mixer_cd(h, bsz, t_len, start, s_rg, s_rgconv, s_ret, p, wb):
    sq = _SeqLayout(bsz, t_len, CHUNK_D)
    hp = _pad_rows(sq, h)
    u = _matmul(hp, wb['w_in1'], IN1, F32, min(sq.m_pad, 1024), 1024)
    y_rg, rg_new = _rglru(sq, u, s_rgconv, s_rg, p, first_pos_is_zero=(start == 0))
    o_d, ret_new = _retention(sq, u, s_ret, start)
    rgconv_new = _last_inputs(sq, u, 0, RG_WIDTH)
    return _real_rows(sq, y_rg), _real_rows(sq, o_d), (rg_new, rgconv_new, ret_new)


def _trunk(x, mod_all, start, states, p, wb):
    s_hgrn, s_delta, s_dconv, s_rg, s_rgconv, s_ret = states
    bsz, t_len, _ = x.shape
    lay = _RowLayout(bsz, t_len)
    m = lay.m
    tm_mm = min(m, 1024)
    tm_ffn = min(m, 2048)
    tm_out = min(m, 512)
    x = x.reshape(m, D_MODEL)

    mods = [lay.mod_arg(mod_all[layer]) for layer in range(DEPTH)]
    h = _modulate(lay, x, mods[0], 0)
    for layer in range(DEPTH):
        lg, lb = p['ln_g'][layer], p['ln_b'][layer]
        mid = _ffn_in(h, wb['ffn_w_in'], layer, 0, tm_ffn)
        ffn = _matmul(mid, wb['ffn_w_out'], D_MODEL, F32, tm_out, 512, lead=(layer, 0))
        x, h = _post_norm(lay, x, ffn, mods[layer], 0, lg[0], lb[0], 0.5, (mods[layer], 1))
        if layer % 2 == 0:
            o_1, o_2, (s_hgrn, s_delta, s_dconv) = _mixer_ab(h, bsz, t_len, layer, s_hgrn, s_delta, s_dconv, p, wb)
            w_o = wb['w_out0']
        else:
            o_1, o_2, (s_rg, s_rgconv, s_ret) = _mixer_cd(h, bsz, t_len, start, s_rg, s_rgconv, s_ret, p, wb)
            w_o = wb['w_out1']
        mix = _matmul_halves(o_1, o_2, w_o, tm_mm, 1024)
        x, h = _post_norm(lay, x, mix, mods[layer], 1, lg[1], lb[1], 1.0, (mods[layer], 2))
        mid = _ffn_in(h, wb['ffn_w_in'], layer, 1, tm_ffn)
        ffn = _matmul(mid, wb['ffn_w_out'], D_MODEL, F32, tm_out, 512, lead=(layer, 1))
        nxt = (mods[layer + 1], 0) if layer + 1 < DEPTH else None
        x, h = _post_norm(lay, x, ffn, mods[layer], 2, lg[2], lb[2], 0.5, nxt)
    return x.reshape(bsz, t_len, D_MODEL), (s_hgrn, s_delta, s_dconv, s_rg, s_rgconv, s_ret)


def kernel(x_prompt, x_sample, c_prompt, c_sample, state_hgrn, state_delta, state_delta_conv, state_rglru, state_rglru_conv, state_ret, ada_w, ada_b, ln_g, ln_b, ffn_w_in, ffn_w_out, w_in0, w_out0, hgrn_lb_logits, hgrn_norm_w, delta_conv_w, delta_a_log, delta_dt_bias, delta_norm_w, w_in1, w_out1, rg_conv_w, rg_conv_b, rg_wa, rg_ba, rg_wx, rg_bx, rg_lambda):
    p = {
        'ln_g': ln_g, 'ln_b': ln_b, 'hgrn_lb_logits': hgrn_lb_logits, 'hgrn_norm_w': hgrn_norm_w,
        'delta_conv_w': delta_conv_w, 'delta_a_log': delta_a_log, 'delta_dt_bias': delta_dt_bias,
        'delta_norm_w': delta_norm_w, 'rg_conv_w': rg_conv_w, 'rg_conv_b': rg_conv_b, 'rg_wa': rg_wa,
        'rg_ba': rg_ba, 'rg_wx': rg_wx, 'rg_bx': rg_bx, 'rg_lambda': rg_lambda,
    }
    wb = {
        'ffn_w_in': ffn_w_in,
        'ffn_w_out': ffn_w_out.astype(BF16),
        'w_in0': w_in0.astype(BF16),
        'w_in0_gates': _delta_gate_weights(w_in0),
        'w_out0': w_out0.astype(BF16),
        'w_in1': w_in1.astype(BF16),
        'w_out1': w_out1.astype(BF16),
    }
    nb, ns = x_prompt.shape[0], x_sample.shape[0]
    c_all = jnp.concatenate([c_prompt, c_sample], axis=0).astype(F32)
    rows = nb + ns
    rows_pad = -(-rows // 16) * 16
    c_act = jnp.pad(jax.nn.silu(c_all), ((0, rows_pad - rows), (0, 0))).astype(BF16)
    mod = _ada(c_act, ada_w, ada_b)
    mod_prompt, mod_sample = mod[:, :nb], mod[:, nb:rows]

    prompt_states = (
        jnp.zeros((nb, H_A, DK_A, DV_A), F32),
        jnp.zeros((nb, H_B, DK_B, DV_B), F32),
        jnp.zeros((nb, CONV_W - 1, C_B), F32),
        jnp.zeros((nb, RG_WIDTH), F32),
        jnp.zeros((nb, CONV_W - 1, RG_WIDTH), F32),
        jnp.zeros((nb, H_D, DK_D, DV_D), F32),
    )
    y_prompt, ps = _trunk(x_prompt, mod_prompt, 0, prompt_states, p, wb)
    sample_states = (state_hgrn, state_delta, state_delta_conv, state_rglru, state_rglru_conv, state_ret)
    y_sample, ss = _trunk(x_sample, mod_sample, PAST_LEN, sample_states, p, wb)
    return (y_prompt, y_sample) + tuple(ps) + tuple(ss)
```
